```python
import math
import jax, jax.numpy as jnp
from jax import lax
import numpy as np

D_MODEL = 1024
BATCH = 32
SEQ = 2048
DEPTH = 4
DEC_BATCH = 4
DEC_SEQ = 4096
PAST_LEN = 128

HEAD_DIM = 64
MIX_WIDTH = D_MODEL
A_HEADS = 4
A_DH = HEAD_DIM // 2
B_HEADS = 8
B_KV_HEADS = 2
B_GROUP = B_HEADS // B_KV_HEADS
C_HEADS = 4
C_DK = 32
C_DV = 64
C_GATE_RANK = 16
GATE_TAU = 16.0
GLA_CHUNK = 64
Q_BLOCK = 128
GRID_W = 64
ROPE_THETA = 10000.0
PLE_DIM = 256
N_GROUPS = 4
EXPERTS_PER_GROUP = 4
N_EXPERTS = N_GROUPS * EXPERTS_PER_GROUP
TOP_K_IN_GROUP = 2
D_EXPERT = 512
EPS = 1e-6

SPLIT_SIZES = (
    A_HEADS * 2 * A_DH, A_HEADS * 2 * A_DH, A_HEADS * HEAD_DIM,
    B_HEADS * HEAD_DIM, B_KV_HEADS * HEAD_DIM, B_KV_HEADS * HEAD_DIM,
    C_HEADS * C_DK, C_HEADS * C_DK, C_HEADS * C_DV, C_HEADS * C_DV,
    2 * C_GATE_RANK,
)
IN_COLS = 256 + 256 + 256 + 512 + 128 + 128 + 128 + 128 + 256 + 256 + 32

kernel_name = 'hymba_style_hybrid_encoder'


def _rms_norm(x, g):
    xf = x.astype(jnp.float32)
    y = xf * lax.rsqrt(jnp.mean(xf * xf, axis=-1, keepdims=True) + EPS)
    return (y * g.astype(jnp.float32)).astype(x.dtype)


def _rope_table(pos, dim):
    inv = 1.0 / (ROPE_THETA ** (jnp.arange(0, dim, 2, dtype=jnp.float32) / dim))
    ang = pos.astype(jnp.float32)[:, None] * inv[None, :]
    return jnp.cos(ang), jnp.sin(ang)


def _apply_rope(x, cos, sin):
    xf = x.astype(jnp.float32)
    x1, x2 = jnp.split(xf, 2, axis=-1)
    c = cos[None, :, None, :]
    s = sin[None, :, None, :]
    return jnp.concatenate([x1 * c - x2 * s, x2 * c + x1 * s], axis=-1).astype(x.dtype)


def _apply_axial_rope(x, rope_row, rope_col):
    half = x.shape[-1] // 2
    return jnp.concatenate([_apply_rope(x[..., :half], rope_row[0], rope_row[1]),
                            _apply_rope(x[..., half:], rope_col[0], rope_col[1])], axis=-1)


def _diff_attention(q, k, v, lam):
    b, s_len, h, _, dh = q.shape
    nb = s_len // Q_BLOCK
    scale = dh ** -0.5
    qb = q.reshape(b, nb, Q_BLOCK, h, 2, dh).swapaxes(0, 1)

    def block(qi):
        s = jnp.einsum('bqhcd,bkhcd->bchqk', qi, k, preferred_element_type=jnp.float32) * scale
        p = jax.nn.softmax(s, axis=-1)
        pd = p[:, 0] - lam * p[:, 1]
        return jnp.einsum('bhqk,bkhe->bqhe', pd.astype(v.dtype), v)

    o = lax.map(block, qb)
    return o.swapaxes(0, 1).reshape(b, s_len, h, v.shape[-1])


def _gqa_attention(q, k, v):
    b, s_len, hq, d = q.shape
    hkv = k.shape[2]
    grp = hq // hkv
    nb = s_len // Q_BLOCK
    scale = d ** -0.5
    qb = q.reshape(b, nb, Q_BLOCK, hkv, grp, d).swapaxes(0, 1)

    def block(qi):
        s = jnp.einsum('bqgrd,bkgd->bgrqk', qi, k, preferred_element_type=jnp.float32) * scale
        p = jax.nn.softmax(s, axis=-1)
        return jnp.einsum('bgrqk,bkgd->bqgrd', p.astype(v.dtype), v)

    o = lax.map(block, qb)
    return o.swapaxes(0, 1).reshape(b, s_len, hq * d)


def _gla_chunked(q, k, v, log_a):
    b, s_len, h, dk = q.shape
    dv = v.shape[-1]
    n = s_len // GLA_CHUNK

    def chunks(t):
        return t.reshape(b, n, GLA_CHUNK, h, t.shape[-1]).transpose(1, 0, 3, 2, 4)

    qc, kc, vc, gc = chunks(q), chunks(k), chunks(v), chunks(log_a)
    bc = jnp.cumsum(gc, axis=-2)
    lower = jnp.tril(jnp.ones((GLA_CHUNK, GLA_CHUNK), dtype=bool))

    def step(state, inp):
        qi, ki, vi, bi = inp
        rel = bi[:, :, :, None, :] - bi[:, :, None, :, :]
        decay = jnp.exp(jnp.where(lower[:, :, None], rel, -jnp.inf))
        att = jnp.einsum('bhid,bhjd,bhijd->bhij', qi, ki, decay)
        o = jnp.einsum('bhij,bhjv->bhiv', att, vi) + \
            jnp.einsum('bhid,bhdv->bhiv', qi * jnp.exp(bi), state)
        blast = bi[:, :, -1, :]
        state = jnp.exp(blast)[..., None] * state + \
            jnp.einsum('bhjd,bhjv->bhdv', ki * jnp.exp(blast[:, :, None, :] - bi), vi)
        return state, o

    state0 = jnp.zeros((b, h, dk, dv), jnp.float32)
    _, o = lax.scan(step, state0, (qc, kc, vc, bc))
    return o.transpose(1, 0, 3, 2, 4).reshape(b, s_len, h, dv)


def _hier_moe(x, w_rg, w_re, w_e1, w_e3, w_e2):
    shp = x.shape
    xt = x.reshape(-1, shp[-1])
    lg = jnp.einsum('nd,dg->ng', xt, w_rg, preferred_element_type=jnp.float32)
    pg = jax.nn.softmax(lg, axis=-1)
    p_top, g_top = lax.top_k(pg, 1)
    le = jnp.einsum('nd,gde->nge', xt, w_re, preferred_element_type=jnp.float32)
    le_sel = jnp.take_along_axis(le, g_top[:, :, None], axis=1)[:, 0]
    ev, ei = lax.top_k(le_sel, TOP_K_IN_GROUP)
    pe = jax.nn.softmax(ev, axis=-1) * p_top
    e_global = g_top * EXPERTS_PER_GROUP + ei
    comb = jnp.sum(jax.nn.one_hot(e_global, N_EXPERTS, dtype=jnp.float32) * pe[..., None], axis=1)
    comb = comb.astype(x.dtype)
    y = jnp.zeros_like(xt)
    for e in range(N_EXPERTS):
        hid = jax.nn.silu(xt @ w_e1[e]) * (xt @ w_e3[e])
        y = y + comb[:, e:e + 1] * (hid @ w_e2[e])
    return y.reshape(shp)


def _layer(h, p_l, l, rope_1d, rope_row, rope_col, g_mix, w_in, a_q_norm, a_k_norm,
           a_lambda, a_subln, b_q_norm, b_k_norm, c_gate_w, c_gate_b, c_out_norm, w_out,
           g_ffn, w_rg, w_re, w_e1, w_e3, w_e2, g_ple, w_pg, w_pp):
    bsz, s_len, _ = h.shape
    a = _rms_norm(h, g_mix)
    proj = a @ w_in
    parts = []
    off = 0
    for n in SPLIT_SIZES:
        parts.append(proj[..., off:off + n])
        off += n
    aq, ak, av, bq, bk, bv, cq, ck, cv, cg, clr = parts

    qa = _rms_norm(aq.reshape(bsz, s_len, A_HEADS * 2, A_DH), a_q_norm)
    qa = _apply_rope(qa, rope_1d[0], rope_1d[1]).reshape(bsz, s_len, A_HEADS, 2, A_DH)
    ka = _rms_norm(ak.reshape(bsz, s_len, A_HEADS * 2, A_DH), a_k_norm)
    ka = _apply_rope(ka, rope_1d[0], rope_1d[1]).reshape(bsz, s_len, A_HEADS, 2, A_DH)
    va = av.reshape(bsz, s_len, A_HEADS, HEAD_DIM)
    lam_init = 0.8 - 0.6 * math.exp(-0.3 * l)
    lf = a_lambda.astype(jnp.float32)
    lam = jnp.exp(jnp.sum(lf[0] * lf[1])) - jnp.exp(jnp.sum(lf[2] * lf[3])) + lam_init
    oa = _diff_attention(qa, ka, va, lam)
    oa = (_rms_norm(oa, a_subln) * (1.0 - lam_init)).reshape(bsz, s_len, A_HEADS * HEAD_DIM)

    qb = _rms_norm(bq.reshape(bsz, s_len, B_HEADS, HEAD_DIM), b_q_norm)
    qb = _apply_axial_rope(qb, rope_row, rope_col)
    kb = _rms_norm(bk.reshape(bsz, s_len, B_KV_HEADS, HEAD_DIM), b_k_norm)
    kb = _apply_axial_rope(kb, rope_row, rope_col)
    vb = bv.reshape(bsz, s_len, B_KV_HEADS, HEAD_DIM)
    ob = _gqa_attention(qb, kb, vb)

    cqf = cq.reshape(bsz, s_len, C_HEADS, C_DK).astype(jnp.float32) * (C_DK ** -0.5)
    ckf = ck.reshape(bsz, s_len, C_HEADS, C_DK).astype(jnp.float32)
    cvf = cv.reshape(bsz, s_len, C_HEADS, C_DV).astype(jnp.float32)
    lr_f = clr[..., :C_GATE_RANK]
    lr_b = clr[..., C_GATE_RANK:]
    la_f = (jax.nn.log_sigmoid((lr_f @ c_gate_w[0] + c_gate_b[0]).astype(jnp.float32)) / GATE_TAU
            ).reshape(bsz, s_len, C_HEADS, C_DK)
    la_b = (jax.nn.log_sigmoid((lr_b @ c_gate_w[1] + c_gate_b[1]).astype(jnp.float32)) / GATE_TAU
            ).reshape(bsz, s_len, C_HEADS, C_DK)
    o_f = _gla_chunked(cqf, ckf, cvf, la_f)
    o_b = jnp.flip(_gla_chunked(jnp.flip(cqf, 1), jnp.flip(ckf, 1), jnp.flip(cvf, 1),
                                jnp.flip(la_b, 1)), 1)
    gate_c = jax.nn.silu(cg.astype(jnp.float32)).reshape(bsz, s_len, C_HEADS, C_DV)
    oc = (_rms_norm(o_f + o_b, c_out_norm) * gate_c).reshape(bsz, s_len, C_HEADS * C_DV)
    oc = oc.astype(h.dtype)

    mix = jnp.concatenate([oa, ob, oc], axis=-1)
    h = h + mix @ w_out

    h = h + _hier_moe(_rms_norm(h, g_ffn), w_rg, w_re, w_e1, w_e3, w_e2)

    gate = jax.nn.sigmoid(_rms_norm(h, g_ple) @ w_pg)
    h = h + (p_l @ w_pp) * gate
    return h


def _trunk(x, p, g_mix, w_in, a_q_norm, a_k_norm, a_lambda, a_subln, b_q_norm, b_k_norm,
           c_gate_w, c_gate_b, c_out_norm, w_out, g_ffn, w_router_group, w_router_expert,
           w_exp_gate, w_exp_up, w_exp_down, g_ple, w_ple_gate, w_ple_proj):
    s_len = x.shape[1]
    rows = s_len // GRID_W
    pos = jnp.arange(s_len)
    row = jnp.repeat(jnp.arange(rows), GRID_W)
    col = jnp.tile(jnp.arange(GRID_W), rows)
    rope_1d = _rope_table(pos, A_DH)
    rope_row = _rope_table(row, HEAD_DIM // 2)
    rope_col = _rope_table(col, HEAD_DIM // 2)
    h = x
    for l in range(DEPTH):
        h = _layer(h, p[l], l, rope_1d, rope_row, rope_col, g_mix[l], w_in[l], a_q_norm[l],
                   a_k_norm[l], a_lambda[l], a_subln[l], b_q_norm[l], b_k_norm[l],
                   c_gate_w[l], c_gate_b[l], c_out_norm[l], w_out[l], g_ffn[l],
                   w_router_group[l], w_router_expert[l], w_exp_gate[l], w_exp_up[l],
                   w_exp_down[l], g_ple[l], w_ple_gate[l], w_ple_proj[l])
    return h


def setup_inputs(seed: int = 0) -> dict:
    key = jax.random.key(seed)
    ks = jax.random.split(key, 32)

    def nrm(k, shape, scale):
        return jax.random.normal(k, shape, jnp.float32) * scale

    def gain(k, shape):
        return 1.0 + 0.1 * jax.random.normal(k, shape, jnp.float32)

    return {
        'x_prompt': nrm(ks[0], (BATCH, SEQ, D_MODEL), 1.0),
        'x_sample': nrm(ks[1], (DEC_BATCH, DEC_SEQ, D_MODEL), 1.0),
        'p_prompt': nrm(ks[2], (DEPTH, BATCH, SEQ, PLE_DIM), 1.0),
        'p_sample': nrm(ks[3], (DEPTH, DEC_BATCH, DEC_SEQ, PLE_DIM), 1.0),
        'g_mix': gain(ks[4], (DEPTH, D_MODEL)),
        'w_in': nrm(ks[5], (DEPTH, D_MODEL, IN_COLS), D_MODEL ** -0.5),
        'a_q_norm': gain(ks[6], (DEPTH, A_DH)),
        'a_k_norm': gain(ks[7], (DEPTH, A_DH)),
        'a_lambda': nrm(ks[8], (DEPTH, 4, A_DH), 0.1),
        'a_subln': gain(ks[9], (DEPTH, HEAD_DIM)),
        'b_q_norm': gain(ks[10], (DEPTH, HEAD_DIM)),
        'b_k_norm': gain(ks[11], (DEPTH, HEAD_DIM)),
        'c_gate_w': nrm(ks[12], (DEPTH, 2, C_GATE_RANK, C_HEADS * C_DK), C_GATE_RANK ** -0.5),
        'c_gate_b': nrm(ks[13], (DEPTH, 2, C_HEADS * C_DK), 0.1),
        'c_out_norm': gain(ks[14], (DEPTH, C_DV)),
        'w_out': nrm(ks[15], (DEPTH, MIX_WIDTH, D_MODEL), MIX_WIDTH ** -0.5),
        'g_ffn': gain(ks[16], (DEPTH, D_MODEL)),
        'w_router_group': nrm(ks[17], (DEPTH, D_MODEL, N_GROUPS), D_MODEL ** -0.5),
        'w_router_expert': nrm(ks[18], (DEPTH, N_GROUPS, D_MODEL, EXPERTS_PER_GROUP), D_MODEL ** -0.5),
        'w_exp_gate': nrm(ks[19], (DEPTH, N_EXPERTS, D_MODEL, D_EXPERT), D_MODEL ** -0.5),
        'w_exp_up': nrm(ks[20], (DEPTH, N_EXPERTS, D_MODEL, D_EXPERT), D_MODEL ** -0.5),
        'w_exp_down': nrm(ks[21], (DEPTH, N_EXPERTS, D_EXPERT, D_MODEL), D_EXPERT ** -0.5),
        'g_ple': gain(ks[22], (DEPTH, D_MODEL)),
        'w_ple_gate': nrm(ks[23], (DEPTH, D_MODEL, D_MODEL), D_MODEL ** -0.5),
        'w_ple_proj': nrm(ks[24], (DEPTH, PLE_DIM, D_MODEL), PLE_DIM ** -0.5),
    }


def reference(x_prompt, x_sample, p_prompt, p_sample, g_mix, w_in, a_q_norm, a_k_norm,
              a_lambda, a_subln, b_q_norm, b_k_norm, c_gate_w, c_gate_b, c_out_norm, w_out,
              g_ffn, w_router_group, w_router_expert, w_exp_gate, w_exp_up, w_exp_down,
              g_ple, w_ple_gate, w_ple_proj):
    y_prompt = _trunk(x_prompt, p_prompt, g_mix, w_in, a_q_norm, a_k_norm, a_lambda, a_subln,
                      b_q_norm, b_k_norm, c_gate_w, c_gate_b, c_out_norm, w_out, g_ffn,
                      w_router_group, w_router_expert, w_exp_gate, w_exp_up, w_exp_down,
                      g_ple, w_ple_gate, w_ple_proj)
    y_sample = _trunk(x_sample, p_sample, g_mix, w_in, a_q_norm, a_k_norm, a_lambda, a_subln,
                      b_q_norm, b_k_norm, c_gate_w, c_gate_b, c_out_norm, w_out, g_ffn,
                      w_router_group, w_router_expert, w_exp_gate, w_exp_up, w_exp_down,
                      g_ple, w_ple_gate, w_ple_proj)
    return (y_prompt, y_sample)
```

```python
import functools
import math

import jax
import jax.numpy as jnp
import numpy as np
from jax import lax
from jax.experimental import pallas as pl
from jax.experimental.pallas import tpu as pltpu

D_MODEL = 1024
DEPTH = 4
HEAD_DIM = 64
A_HEADS = 4
A_DH = 32
B_HEADS = 8
B_KV_HEADS = 2
C_HEADS = 4
C_DK = 32
C_DV = 64
C_GATE_RANK = 16
GATE_TAU = 16.0
GLA_CHUNK = 64
GLA_SUB = 16
GRID_W = 64
ROPE_THETA = 10000.0
PLE_DIM = 256
N_GROUPS = 4
EXPERTS_PER_GROUP = 4
N_EXPERTS = 16
D_EXPERT = 512
EPS = 1e-6
LOG2E = 1.4426950408889634

VMEM_LIMIT_BYTES = 52 * 1024 * 1024
NEG_INF = float("-inf")

QK_COLS = 1152
MAIN_COLS = 2048
V_ROWS = 384
ROUTE_LANES = 128
ROUTE_OFF = 4


def _cparams(sem):
    return pltpu.CompilerParams(dimension_semantics=sem, vmem_limit_bytes=VMEM_LIMIT_BYTES)


def _bf(x):
    return x.astype(jnp.bfloat16)


def _dot(a, b):
    return jnp.dot(a, b, preferred_element_type=jnp.float32)


def _dot_nt(a, b):
    return lax.dot_general(a, b, (((1,), (1,)), ((), ())), preferred_element_type=jnp.float32)


def _dot_tn(a, b):
    return lax.dot_general(a, b, (((0,), (0,)), ((), ())), preferred_element_type=jnp.float32)


def _iota(shape, dim):
    return lax.broadcasted_iota(jnp.int32, shape, dim)


def _in_proj_kernel(h_ref, gmix_ref, w_ref, wvt_ref, gn_ref, cosa_ref, sina_ref, cosb_ref, sinb_ref,
                    m32_ref, m64_ref, rot_ref, gw_ref, gb_ref,
                    qa_ref, ka_ref, qb_ref, kb_ref, vt_ref, cqk_ref, la_ref, cv_ref, cg_ref):
    x = h_ref[...]
    a = x * lax.rsqrt(jnp.mean(x * x, axis=-1, keepdims=True) + EPS) * gmix_ref[...]
    ab = _bf(a)
    proj = _dot(ab, w_ref[...])
    vt_ref[...] = _bf(_dot_nt(wvt_ref[...], ab))

    cosa = cosa_ref[...]
    sina = sina_ref[...]
    cosb = cosb_ref[...]
    sinb = sinb_ref[...]
    cosa2 = jnp.concatenate([cosa, cosa], axis=1)
    sina2 = jnp.concatenate([sina, sina], axis=1)
    cosb2 = jnp.concatenate([cosb, cosb], axis=1)
    sinb2 = jnp.concatenate([sinb, sinb], axis=1)

    def qk_slab(off, width, mean_mat, cos, sin, out_ref, out_off):
        xs = proj[:, off:off + width]
        ms = _dot(_bf(xs * xs), mean_mat)
        xn = xs * lax.rsqrt(ms + EPS) * gn_ref[:, off:off + width]
        rot = _dot(_bf(xn), rot_ref[0:width, 0:width])
        out_ref[:, out_off:out_off + width] = _bf(xn * cos + rot * sin)

    m32 = m32_ref[...]
    m64 = m64_ref[...]
    qk_slab(0, 256, m32, cosa2, sina2, qa_ref, 0)
    qk_slab(256, 256, m32, cosa2, sina2, ka_ref, 0)
    qk_slab(512, 256, m64, cosb2, sinb2, qb_ref, 0)
    qk_slab(768, 256, m64, cosb2, sinb2, qb_ref, 256)
    qk_slab(1024, 128, m64_ref[0:128, 0:128], cosb, sinb, kb_ref, 0)

    c0 = QK_COLS
    cqk_ref[:, 0:128] = proj[:, c0:c0 + 128] * (C_DK ** -0.5)
    cqk_ref[:, 128:256] = proj[:, c0 + 128:c0 + 256]
    cv_ref[...] = proj[:, c0 + 256:c0 + 512]
    cg = proj[:, c0 + 512:c0 + 768]
    cg_ref[...] = cg * (1.0 / (1.0 + jnp.exp(-cg)))
    clr = proj[:, c0 + 768:c0 + 896]
    z = _dot(_bf(clr), gw_ref[...]) + gb_ref[...]
    la_ref[...] = (jnp.minimum(z, 0.0) - jnp.log(1.0 + jnp.exp(-jnp.abs(z)))) * (1.0 / GATE_TAU)


def _in_proj(h, lw, tabs, consts, seq, tm):
    n = h.shape[0]
    nt = n // tm
    spb = seq // tm

    def row(i):
        return (i, 0)

    def full(i):
        return (0, 0)

    def tab(i):
        return (i % spb, 0)

    f32 = jnp.float32
    bf16 = jnp.bfloat16
    out_shape = (
        jax.ShapeDtypeStruct((n, 256), bf16),
        jax.ShapeDtypeStruct((n, 256), bf16),
        jax.ShapeDtypeStruct((n, 512), bf16),
        jax.ShapeDtypeStruct((n, 128), bf16),
        jax.ShapeDtypeStruct((V_ROWS, n), bf16),
        jax.ShapeDtypeStruct((n, 256), f32),
        jax.ShapeDtypeStruct((n, 256), f32),
        jax.ShapeDtypeStruct((n, 256), f32),
        jax.ShapeDtypeStruct((n, 256), f32),
    )
    in_specs = [
        pl.BlockSpec((tm, D_MODEL), row),
        pl.BlockSpec((1, D_MODEL), full),
        pl.BlockSpec((D_MODEL, MAIN_COLS), full),
        pl.BlockSpec((V_ROWS, D_MODEL), full),
        pl.BlockSpec((1, QK_COLS), full),
        pl.BlockSpec((tm, 128), tab),
        pl.BlockSpec((tm, 128), tab),
        pl.BlockSpec((tm, 128), tab),
        pl.BlockSpec((tm, 128), tab),
        pl.BlockSpec((256, 256), full),
        pl.BlockSpec((256, 256), full),
        pl.BlockSpec((256, 256), full),
        pl.BlockSpec((128, 256), full),
        pl.BlockSpec((1, 256), full),
    ]
    out_specs = (
        pl.BlockSpec((tm, 256), row),
        pl.BlockSpec((tm, 256), row),
        pl.BlockSpec((tm, 512), row),
        pl.BlockSpec((tm, 128), row),
        pl.BlockSpec((V_ROWS, tm), lambda i: (0, i)),
        pl.BlockSpec((tm, 256), row),
        pl.BlockSpec((tm, 256), row),
        pl.BlockSpec((tm, 256), row),
        pl.BlockSpec((tm, 256), row),
    )
    return pl.pallas_call(
        _in_proj_kernel, out_shape=out_shape, grid=(nt,), in_specs=in_specs, out_specs=out_specs,
        compiler_params=_cparams(("parallel",)), name="in_proj",
    )(h, lw["g_mix"], lw["w_main"], lw["w_vt"], lw["gn"], tabs["cosa"], tabs["sina"], tabs["cosb"],
      tabs["sinb"], consts["m32"], consts["m64"], consts["rot"], lw["gw"], lw["gb"])


def _softmax_pv(k, qm, vt):
    st = _dot_nt(k, qm)
    m = jnp.max(st, axis=0, keepdims=True)
    p = jnp.exp2(st - m)
    l = jnp.sum(p, axis=0, keepdims=True)
    ot = _dot(vt, _bf(p))
    return ot / l


def _diff_attn_kernel(q_ref, k_ref, vt_ref, lam_ref, gsub_ref, o_ref):
    q = q_ref[...]
    k = k_ref[...]
    lane = _iota(q.shape, 1)
    zero = jnp.zeros_like(q)
    lam = lam_ref[...]
    gsub = gsub_ref[...]
    outs = []
    for h in range(A_HEADS):
        vt = vt_ref[HEAD_DIM * h:HEAD_DIM * (h + 1), :]
        comp = []
        for c in range(2):
            lo = A_DH * (2 * h + c)
            qm = jnp.where((lane >= lo) & (lane < lo + A_DH), q, zero)
            comp.append(_softmax_pv(k, qm, vt))
        od = comp[0] - lam * comp[1]
        r = lax.rsqrt(jnp.mean(od * od, axis=0, keepdims=True) + EPS)
        outs.append(od * r * gsub)
    ot = jnp.concatenate(outs, axis=0)
    o_ref[...] = _bf(ot.T)


def _gqa_attn_kernel(q_ref, k_ref, vt_ref, o_ref):
    k = k_ref[...]
    outs = []
    for j in range(B_HEADS // B_KV_HEADS):
        q = q_ref[:, 128 * j:128 * (j + 1)]
        lane = _iota(q.shape, 1)
        zero = jnp.zeros_like(q)
        for g in range(B_KV_HEADS):
            qm = jnp.where((lane >= HEAD_DIM * g) & (lane < HEAD_DIM * (g + 1)), q, zero)
            vt = vt_ref[HEAD_DIM * g:HEAD_DIM * (g + 1), :]
            outs.append(_softmax_pv(k, qm, vt))
    ot = jnp.concatenate(outs, axis=0)
    o_ref[...] = _bf(ot.T)


def _diff_attn(qa, ka, vt, lam, gsub, batch, seq, tq):
    n = qa.shape[0]
    nq = seq // tq
    return pl.pallas_call(
        _diff_attn_kernel,
        out_shape=jax.ShapeDtypeStruct((n, 256), jnp.bfloat16),
        grid=(batch, nq),
        in_specs=[
            pl.BlockSpec((tq, 256), lambda b, i: (b * nq + i, 0)),
            pl.BlockSpec((seq, 256), lambda b, i: (b, 0)),
            pl.BlockSpec((256, seq), lambda b, i: (0, b)),
            pl.BlockSpec((1, 1), lambda b, i: (0, 0)),
            pl.BlockSpec((HEAD_DIM, 1), lambda b, i: (0, 0)),
        ],
        out_specs=pl.BlockSpec((tq, 256), lambda b, i: (b * nq + i, 0)),
        compiler_params=_cparams(("parallel", "parallel")), name="diff_attn",
    )(qa, ka, vt, lam, gsub)


def _gqa_attn(qb, kb, vt, batch, seq, tq):
    n = qb.shape[0]
    nq = seq // tq
    return pl.pallas_call(
        _gqa_attn_kernel,
        out_shape=jax.ShapeDtypeStruct((n, 512), jnp.bfloat16),
        grid=(batch, nq),
        in_specs=[
            pl.BlockSpec((tq, 512), lambda b, i: (b * nq + i, 0)),
            pl.BlockSpec((seq, 128), lambda b, i: (b, 0)),
            pl.BlockSpec((128, seq), lambda b, i: (2, b)),
        ],
        out_specs=pl.BlockSpec((tq, 512), lambda b, i: (b * nq + i, 0)),
        compiler_params=_cparams(("parallel", "parallel")), name="gqa_attn",
    )(qb, kb, vt)


def _split3(x):
    hi = _bf(x)
    r1 = x - hi.astype(jnp.float32)
    mid = _bf(r1)
    lo = _bf(r1 - mid.astype(jnp.float32))
    return hi, mid, lo


def _gla_kernel(qk_ref, g_ref, v_ref, o_ref, st_ref, *, nchunk, rev):
    C = GLA_CHUNK
    SB = GLA_SUB
    nsub = C // SB

    @pl.when(pl.program_id(1) == 0)
    def _():
        st_ref[...] = jnp.zeros_like(st_ref)

    r64 = _iota((C, C), 0)
    c64 = _iota((C, C), 1)
    tri = _bf(jnp.where((c64 >= r64) if rev else (c64 <= r64), 1.0, 0.0))
    hm_k = (_iota((4 * C, 128), 0) >> 6) == (_iota((4 * C, 128), 1) >> 5)
    hm_v = (_iota((4 * C, 256), 0) >> 6) == (_iota((4 * C, 256), 1) >> 6)
    ones_exp = _bf(jnp.where((_iota((128, 256), 0) >> 5) == (_iota((128, 256), 1) >> 6), 1.0, 0.0))
    rowc = _iota((C, 128), 0)
    rows = _iota((SB, 128), 0)

    order = range(nchunk - 1, -1, -1) if rev else range(nchunk)
    for ci in order:
        r0 = ci * C
        q = qk_ref[r0:r0 + C, 0:128]
        k = qk_ref[r0:r0 + C, 128:256]
        g = g_ref[r0:r0 + C, :]
        v = v_ref[r0:r0 + C, :]
        g1, g2, g3 = _split3(g)
        b = _dot(tri, g1) + _dot(tri, g2) + _dot(tri, g3)

        st = st_ref[...]
        o = _dot_nt(_bf(q * jnp.exp(b)), _bf(st))
        bl = b[0:1, :] if rev else b[C - 1:C, :]
        ke = k * jnp.exp(bl - b)
        dst = _dot_tn(_bf(v), _bf(ke))
        st_ref[...] = st * jnp.exp(bl) + jnp.where(hm_k, dst, 0.0)

        att = jnp.zeros((C, 4 * C), jnp.float32)
        for blk in (range(0, nsub - 1) if rev else range(1, nsub)):
            if rev:
                ref = b[SB * blk + SB - 1:SB * blk + SB, :]
                kmask = rowc >= SB * (blk + 1)
            else:
                ref = b[SB * blk:SB * blk + 1, :]
                kmask = rowc < SB * blk
            qmask = (rowc >= SB * blk) & (rowc < SB * (blk + 1))
            qd = q * jnp.exp(jnp.where(qmask, b - ref, NEG_INF))
            kd = k * jnp.exp(jnp.where(kmask, ref - b, NEG_INF))
            kexp = jnp.where(hm_k, jnp.concatenate([kd] * 4, axis=0), 0.0)
            att = att + _dot_nt(_bf(qd), _bf(kexp))
        vexp = jnp.where(hm_v, jnp.concatenate([v] * 4, axis=0), 0.0)
        o = o + _dot(_bf(att), _bf(vexp))

        diag = []
        for blk in range(nsub):
            s0 = SB * blk
            bs = b[s0:s0 + SB, :]
            qs = q[s0:s0 + SB, :]
            ks = k[s0:s0 + SB, :]
            vs = v[s0:s0 + SB, :]
            ws = []
            for j in range(SB):
                keep = (rows <= j) if rev else (rows >= j)
                e = jnp.exp(jnp.where(keep, bs - bs[j:j + 1, :], NEG_INF))
                ws.append(e * qs * ks[j:j + 1, :])
            p = _dot(_bf(jnp.concatenate(ws, axis=0)), ones_exp)
            od = p[0:SB, :] * vs[0:1, :]
            for j in range(1, SB):
                od = od + p[SB * j:SB * (j + 1), :] * vs[j:j + 1, :]
            diag.append(od)
        o_ref[r0:r0 + C, :] = o + jnp.concatenate(diag, axis=0)


def _gla(cqk, la, cv, batch, seq, tb, rev):
    n = cqk.shape[0]
    nb = seq // tb

    def blk(b, i):
        return (b * nb + (nb - 1 - i if rev else i), 0)

    def gblk(b, i):
        return (b * nb + (nb - 1 - i if rev else i), 1 if rev else 0)

    return pl.pallas_call(
        functools.partial(_gla_kernel, nchunk=tb // GLA_CHUNK, rev=rev),
        out_shape=jax.ShapeDtypeStruct((n, 256), jnp.float32),
        grid=(batch, nb),
        in_specs=[
            pl.BlockSpec((tb, 256), blk),
            pl.BlockSpec((tb, 128), gblk),
            pl.BlockSpec((tb, 256), blk),
        ],
        out_specs=pl.BlockSpec((tb, 256), blk),
        scratch_shapes=[pltpu.VMEM((256, 128), jnp.float32)],
        compiler_params=_cparams(("parallel", "arbitrary")),
        name="gla_bwd" if rev else "gla_fwd",
    )(cqk, la, cv)


def _out_proj_kernel(h_ref, oa_ref, ob_ref, of_ref, obw_ref, cg_ref, gc_ref, m64_ref, wo_ref, gffn_ref,
                     wrh_ref, wrl_ref, h1_ref, xb_ref, comb_ref):
    oc = of_ref[...] + obw_ref[...]
    ms = _dot(_bf(oc * oc), m64_ref[...])
    oc = oc * lax.rsqrt(ms + EPS) * gc_ref[...] * cg_ref[...]
    h1 = (h_ref[...] + _dot(oa_ref[...], wo_ref[0:256, :]) + _dot(ob_ref[...], wo_ref[256:768, :])
          + _dot(_bf(oc), wo_ref[768:1024, :]))
    h1_ref[...] = h1
    xn = h1 * lax.rsqrt(jnp.mean(h1 * h1, axis=-1, keepdims=True) + EPS) * gffn_ref[...]
    xh = _bf(xn)
    xb_ref[...] = xh
    xl = _bf(xn - xh.astype(jnp.float32))
    wrh = wrh_ref[...]
    lg = _dot(xh, wrh) + _dot(xh, wrl_ref[...]) + _dot(xl, wrh)

    lane = _iota(lg.shape, 1)
    big = jnp.int32(1 << 20)
    is_g = lane < N_GROUPS
    mg = jnp.max(jnp.where(is_g, lg, NEG_INF), axis=-1, keepdims=True)
    sg = jnp.sum(jnp.where(is_g, jnp.exp(lg - mg), 0.0), axis=-1, keepdims=True)
    p_top = 1.0 / sg
    g_top = jnp.min(jnp.where(is_g & (lg == mg), lane, big), axis=-1, keepdims=True)
    e_lo = ROUTE_OFF + EXPERTS_PER_GROUP * g_top
    sel = (lane >= e_lo) & (lane < e_lo + EXPERTS_PER_GROUP)
    ev0 = jnp.max(jnp.where(sel, lg, NEG_INF), axis=-1, keepdims=True)
    i0 = jnp.min(jnp.where(sel & (lg == ev0), lane, big), axis=-1, keepdims=True)
    sel1 = sel & (lane != i0)
    ev1 = jnp.max(jnp.where(sel1, lg, NEG_INF), axis=-1, keepdims=True)
    i1 = jnp.min(jnp.where(sel1 & (lg == ev1), lane, big), axis=-1, keepdims=True)
    t = jnp.exp(ev1 - ev0)
    pe0 = p_top / (1.0 + t)
    pe1 = p_top * t / (1.0 + t)
    comb_ref[...] = jnp.where(lane == i0, pe0, jnp.where(lane == i1, pe1, 0.0))


def _out_proj(h, oa, ob, o_f, o_b, cgate, lw, consts, tm):
    n = h.shape[0]

    def row(i):
        return (i, 0)

    def full(i):
        return (0, 0)

    return pl.pallas_call(
        _out_proj_kernel,
        out_shape=(jax.ShapeDtypeStruct((n, D_MODEL), jnp.float32),
                   jax.ShapeDtypeStruct((n, D_MODEL), jnp.bfloat16),
                   jax.ShapeDtypeStruct((n, ROUTE_LANES), jnp.float32)),
        grid=(n // tm,),
        in_specs=[
            pl.BlockSpec((tm, D_MODEL), row),
            pl.BlockSpec((tm, 256), row),
            pl.BlockSpec((tm, 512), row),
            pl.BlockSpec((tm, 256), row),
            pl.BlockSpec((tm, 256), row),
            pl.BlockSpec((tm, 256), row),
            pl.BlockSpec((1, 256), full),
            pl.BlockSpec((256, 256), full),
            pl.BlockSpec((D_MODEL, D_MODEL), full),
            pl.BlockSpec((1, D_MODEL), full),
            pl.BlockSpec((D_MODEL, ROUTE_LANES), full),
            pl.BlockSpec((D_MODEL, ROUTE_LANES), full),
        ],
        out_specs=(pl.BlockSpec((tm, D_MODEL), row), pl.BlockSpec((tm, D_MODEL), row),
                   pl.BlockSpec((tm, ROUTE_LANES), row)),
        compiler_params=_cparams(("parallel",)), name="out_proj",
    )(h, oa, ob, o_f, o_b, cgate, lw["gc"], consts["m64"], lw["w_out"], lw["g_ffn"], lw["wr_hi"],
      lw["wr_lo"])


def _moe_ple_kernel(h1_ref, xb_ref, comb_ref, w1_ref, w3_ref, w2_ref, p_ref, gple_ref, wpg_ref, wpp_ref,
                    o_ref):
    e = pl.program_id(1)

    @pl.when(e == 0)
    def _():
        o_ref[...] = h1_ref[...]

    x = xb_ref[...]
    h1 = _dot(x, w1_ref[0])
    h3 = _dot(x, w3_ref[0])
    hid = h1 * (1.0 / (1.0 + jnp.exp(-h1))) * h3
    y = _dot(_bf(hid), w2_ref[0])
    comb = comb_ref[...]
    lane = _iota(comb.shape, 1)
    ce = jnp.sum(jnp.where(lane == e + ROUTE_OFF, comb, 0.0), axis=-1, keepdims=True)
    o_ref[...] += ce * y

    @pl.when(e == N_EXPERTS - 1)
    def _():
        h2 = o_ref[...]
        xn = h2 * lax.rsqrt(jnp.mean(h2 * h2, axis=-1, keepdims=True) + EPS) * gple_ref[...]
        z = _dot(_bf(xn), wpg_ref[...])
        gate = 1.0 / (1.0 + jnp.exp(-z))
        o_ref[...] = h2 + _dot(_bf(p_ref[...]), wpp_ref[...]) * gate


def _moe_ple(h1, xb, comb, p_l, lw, tm):
    n = h1.shape[0]

    def row(i, e):
        return (i, 0)

    def full(i, e):
        return (0, 0)

    def wexp(i, e):
        return (e, 0, 0)

    return pl.pallas_call(
        _moe_ple_kernel,
        out_shape=jax.ShapeDtypeStruct((n, D_MODEL), jnp.float32),
        grid=(n // tm, N_EXPERTS),
        in_specs=[
            pl.BlockSpec((tm, D_MODEL), row),
            pl.BlockSpec((tm, D_MODEL), row),
            pl.BlockSpec((tm, ROUTE_LANES), row),
            pl.BlockSpec((1, D_MODEL, D_EXPERT), wexp),
            pl.BlockSpec((1, D_MODEL, D_EXPERT), wexp),
            pl.BlockSpec((1, D_EXPERT, D_MODEL), wexp),
            pl.BlockSpec((tm, PLE_DIM), row),
            pl.BlockSpec((1, D_MODEL), full),
            pl.BlockSpec((D_MODEL, D_MODEL), full),
            pl.BlockSpec((PLE_DIM, D_MODEL), full),
        ],
        out_specs=pl.BlockSpec((tm, D_MODEL), row),
        compiler_params=_cparams(("parallel", "arbitrary")), name="moe_ple",
    )(h1, xb, comb, lw["w1"], lw["w3"], lw["w2"], p_l, lw["g_ple"], lw["w_pg"], lw["w_pp"])


_B_HEAD_ORDER = np.array([0, 4, 1, 5, 2, 6, 3, 7])


def _block_mean_matrix(width, block):
    idx = np.arange(width)
    return ((idx[:, None] // block) == (idx[None, :] // block)).astype(np.float32) / block


def _rotate_half_matrix(width, block):
    half = block // 2
    m = np.zeros((width, width), np.float32)
    for j in range(width):
        if j % block < half:
            m[j + half, j] = -1.0
        else:
            m[j - half, j] = 1.0
    return m


def _constants():
    bf16 = jnp.bfloat16
    return {
        "m32": jnp.asarray(_block_mean_matrix(256, A_DH), bf16),
        "m64": jnp.asarray(_block_mean_matrix(256, HEAD_DIM), bf16),
        "rot": jnp.asarray(_rotate_half_matrix(256, A_DH), bf16),
    }


def _rope_tables(seq):
    def table(pos, dim):
        inv = 1.0 / (ROPE_THETA ** (jnp.arange(0, dim, 2, dtype=jnp.float32) / dim))
        ang = pos.astype(jnp.float32)[:, None] * inv[None, :]
        return jnp.cos(ang), jnp.sin(ang)

    rows = seq // GRID_W
    pos = jnp.arange(seq)
    row = jnp.repeat(jnp.arange(rows), GRID_W)
    col = jnp.tile(jnp.arange(GRID_W), rows)
    c1, s1 = table(pos, A_DH)
    cr, sr = table(row, HEAD_DIM // 2)
    cc, sc = table(col, HEAD_DIM // 2)
    return {
        "cosa": jnp.tile(jnp.concatenate([c1, c1], axis=1), (1, 4)),
        "sina": jnp.tile(jnp.concatenate([s1, s1], axis=1), (1, 4)),
        "cosb": jnp.tile(jnp.concatenate([cr, cr, cc, cc], axis=1), (1, 2)),
        "sinb": jnp.tile(jnp.concatenate([sr, sr, sc, sc], axis=1), (1, 2)),
    }


def _prep_layer(l, g_mix, w_in, a_q_norm, a_k_norm, a_lambda, a_subln, b_q_norm, b_k_norm, c_gate_w,
                c_gate_b, c_out_norm, w_out, g_ffn, w_rg, w_re, w_e1, w_e3, w_e2, g_ple, w_pg, w_pp):
    f32 = jnp.float32
    bf16 = jnp.bfloat16
    aq, ak, av = w_in[:, 0:256], w_in[:, 256:512], w_in[:, 512:768]
    bq, bk, bv = w_in[:, 768:1280], w_in[:, 1280:1408], w_in[:, 1408:1536]
    crest = w_in[:, 1536:2336]
    bq = bq.reshape(D_MODEL, B_HEADS, HEAD_DIM)[:, _B_HEAD_ORDER, :].reshape(D_MODEL, 512)
    w_main = jnp.concatenate(
        [aq, ak, bq, bk, crest, jnp.zeros((D_MODEL, MAIN_COLS - QK_COLS - 800), f32)], axis=1)
    w_vt = jnp.concatenate([av, bv], axis=1).T

    gn = jnp.concatenate([
        jnp.tile(a_q_norm, 8) * (A_DH ** -0.5 * LOG2E),
        jnp.tile(a_k_norm, 8),
        jnp.tile(b_q_norm, 8) * (HEAD_DIM ** -0.5 * LOG2E),
        jnp.tile(b_k_norm, 2),
    ])[None, :]

    gw = jnp.zeros((128, 256), f32)
    gw = gw.at[0:C_GATE_RANK, 0:128].set(c_gate_w[0])
    gw = gw.at[C_GATE_RANK:2 * C_GATE_RANK, 128:256].set(c_gate_w[1])
    gb = jnp.concatenate([c_gate_b[0], c_gate_b[1]])[None, :]

    lam_init = 0.8 - 0.6 * math.exp(-0.3 * l)
    lf = a_lambda.astype(f32)
    lam = jnp.exp(jnp.sum(lf[0] * lf[1])) - jnp.exp(jnp.sum(lf[2] * lf[3])) + lam_init

    wo_b = w_out[256:768].reshape(B_HEADS, HEAD_DIM, D_MODEL)[_B_HEAD_ORDER].reshape(512, D_MODEL)
    w_out_p = jnp.concatenate([w_out[0:256], wo_b, w_out[768:1024]], axis=0)

    wr = jnp.zeros((D_MODEL, ROUTE_LANES), f32)
    wr = wr.at[:, 0:N_GROUPS].set(w_rg)
    wr = wr.at[:, ROUTE_OFF:ROUTE_OFF + N_EXPERTS].set(
        jnp.transpose(w_re, (1, 0, 2)).reshape(D_MODEL, N_EXPERTS))
    wr_hi = wr.astype(bf16)
    wr_lo = (wr - wr_hi.astype(f32)).astype(bf16)

    return {
        "g_mix": g_mix[None, :], "w_main": w_main.astype(bf16), "w_vt": w_vt.astype(bf16), "gn": gn,
        "gw": gw.astype(bf16), "gb": gb, "lam": jnp.reshape(lam, (1, 1)),
        "gsub": (a_subln * (1.0 - lam_init))[:, None],
        "gc": jnp.tile(c_out_norm, C_HEADS)[None, :], "w_out": w_out_p.astype(bf16),
        "g_ffn": g_ffn[None, :], "wr_hi": wr_hi, "wr_lo": wr_lo,
        "w1": w_e1.astype(bf16), "w3": w_e3.astype(bf16), "w2": w_e2.astype(bf16),
        "g_ple": g_ple[None, :], "w_pg": w_pg.astype(bf16), "w_pp": w_pp.astype(bf16),
    }


def _tile(limit, total):
    t = min(limit, total)
    assert total % t == 0
    return t


def _trunk(x, p, layers, consts):
    batch, seq, _ = x.shape
    n = batch * seq
    tabs = _rope_tables(seq)
    tm = _tile(512, seq)
    tq = _tile(256, seq)
    tb = _tile(256, seq)
    h = x.reshape(n, D_MODEL)
    for l in range(DEPTH):
        lw = layers[l]
        qa, ka, qb, kb, vt, cqk, la, cv, cgate = _in_proj(h, lw, tabs, consts, seq, tm)
        oa = _diff_attn(qa, ka, vt, lw["lam"], lw["gsub"], batch, seq, tq)
        ob = _gqa_attn(qb, kb, vt, batch, seq, tq)
        o_f = _gla(cqk, la, cv, batch, seq, tb, rev=False)
        o_b = _gla(cqk, la, cv, batch, seq, tb, rev=True)
        h1, xb, comb = _out_proj(h, oa, ob, o_f, o_b, cgate, lw, consts, tm)
        h = _moe_ple(h1, xb, comb, p[l].reshape(n, PLE_DIM), lw, tm)
    return h.reshape(batch, seq, D_MODEL)


def kernel(x_prompt, x_sample, p_prompt, p_sample, g_mix, w_in, a_q_norm, a_k_norm, a_lambda, a_subln, b_q_norm, b_k_norm, c_gate_w, c_gate_b, c_out_norm, w_out, g_ffn, w_router_group, w_router_expert, w_exp_gate, w_exp_up, w_exp_down, g_ple, w_ple_gate, w_ple_proj):
    consts = _constants()
    layers = [
        _prep_layer(l, g_mix[l], w_in[l], a_q_norm[l], a_k_norm[l], a_lambda[l], a_subln[l], b_q_norm[l],
                    b_k_norm[l], c_gate_w[l], c_gate_b[l], c_out_norm[l], w_out[l], g_ffn[l],
                    w_router_group[l], w_router_expert[l], w_exp_gate[l], w_exp_up[l], w_exp_down[l],
                    g_ple[l], w_ple_gate[l], w_ple_proj[l])
        for l in range(DEPTH)
    ]
    y_prompt = _trunk(x_prompt, p_prompt, layers, consts)
    y_sample = _trunk(x_sample, p_sample, layers, consts)
    return (y_prompt, y_sample)
```

```python
import functools
import math

import jax
import jax.numpy as jnp
import numpy as np
from jax import lax
from jax.experimental import pallas as pl
from jax.experimental.pallas import tpu as pltpu

D_MODEL = 1024
DEPTH = 4
HEAD_DIM = 64
A_HEADS = 4
A_DH = 32
B_HEADS = 8
B_KV_HEADS = 2
C_HEADS = 4
C_DK = 32
C_DV = 64
C_GATE_RANK = 16
GATE_TAU = 16.0
GLA_CHUNK = 64
GLA_SUB = 16
GRID_W = 64
ROPE_THETA = 10000.0
PLE_DIM = 256
N_GROUPS = 4
EXPERTS_PER_GROUP = 4
N_EXPERTS = 16
D_EXPERT = 512
EPS = 1e-6
LOG2E = 1.4426950408889634

VMEM_LIMIT_BYTES = 52 * 1024 * 1024
NEG_INF = float("-inf")

QK_COLS = 1152
MAIN_COLS = 2048
VT_HEAD_ROWS = 80
V_ROWS = 384
VT_ROWS = 6 * VT_HEAD_ROWS
TQ_BOUNDED = 512
TQ_SHIFTED = 256
SCORE_BOUND = 60.0
ZERO_ROWS = 256
SUBLANES = 8
ROUTE_LANES = 128
ROUTE_OFF = 4


def _cparams(sem):
    return pltpu.CompilerParams(dimension_semantics=sem, vmem_limit_bytes=VMEM_LIMIT_BYTES)


def _bf(x):
    return x.astype(jnp.bfloat16)


def _dot(a, b):
    return jnp.dot(a, b, preferred_element_type=jnp.float32)


def _dot_nt(a, b):
    return lax.dot_general(a, b, (((1,), (1,)), ((), ())), preferred_element_type=jnp.float32)


def _dot_tn(a, b):
    return lax.dot_general(a, b, (((0,), (0,)), ((), ())), preferred_element_type=jnp.float32)


def _iota(shape, dim):
    return lax.broadcasted_iota(jnp.int32, shape, dim)


def _in_proj_kernel(h_ref, gmix_ref, w_ref, wvt_ref, gn_ref, cosa_ref, sina_ref, cosb_ref, sinb_ref,
                    m32_ref, m64_ref, rot_ref, gw_ref, gb_ref,
                    qa_ref, ka_ref, qb_ref, kb_ref, vt_ref, cqk_ref, la_ref, cv_ref, cg_ref):
    x = h_ref[...]
    a = x * lax.rsqrt(jnp.mean(x * x, axis=-1, keepdims=True) + EPS) * gmix_ref[...]
    ab = _bf(a)
    proj = _dot(ab, w_ref[...])
    vt = _bf(_dot_nt(wvt_ref[...], ab))
    ones = jnp.ones((VT_HEAD_ROWS - HEAD_DIM, vt.shape[1]), jnp.bfloat16)
    for hh in range(VT_ROWS // VT_HEAD_ROWS):
        vt_ref[VT_HEAD_ROWS * hh:VT_HEAD_ROWS * hh + HEAD_DIM, :] = vt[HEAD_DIM * hh:HEAD_DIM * (hh + 1), :]
        vt_ref[VT_HEAD_ROWS * hh + HEAD_DIM:VT_HEAD_ROWS * (hh + 1), :] = ones

    cosa = cosa_ref[...]
    sina = sina_ref[...]
    cosb = cosb_ref[...]
    sinb = sinb_ref[...]
    cosa2 = jnp.concatenate([cosa, cosa], axis=1)
    sina2 = jnp.concatenate([sina, sina], axis=1)
    cosb2 = jnp.concatenate([cosb, cosb], axis=1)
    sinb2 = jnp.concatenate([sinb, sinb], axis=1)

    def qk_slab(off, width, mean_mat, cos, sin, out_ref, out_off):
        xs = proj[:, off:off + width]
        ms = _dot(_bf(xs * xs), mean_mat)
        xn = xs * lax.rsqrt(ms + EPS) * gn_ref[:, off:off + width]
        rot = _dot(_bf(xn), rot_ref[0:width, 0:width])
        out_ref[:, out_off:out_off + width] = _bf(xn * cos + rot * sin)

    m32 = m32_ref[...]
    m64 = m64_ref[...]
    qk_slab(0, 256, m32, cosa2, sina2, qa_ref, 0)
    qk_slab(256, 256, m32, cosa2, sina2, ka_ref, 0)
    qk_slab(512, 256, m64, cosb2, sinb2, qb_ref, 0)
    qk_slab(768, 256, m64, cosb2, sinb2, qb_ref, 256)
    qk_slab(1024, 128, m64_ref[0:128, 0:128], cosb, sinb, kb_ref, 0)

    c0 = QK_COLS
    cqk_ref[:, 0:128] = proj[:, c0:c0 + 128] * (C_DK ** -0.5)
    cqk_ref[:, 128:256] = proj[:, c0 + 128:c0 + 256]
    cv_ref[...] = proj[:, c0 + 256:c0 + 512]
    cg = proj[:, c0 + 512:c0 + 768]
    cg_ref[...] = cg * (1.0 / (1.0 + jnp.exp(-cg)))
    clr = proj[:, c0 + 768:c0 + 896]
    z = _dot(_bf(clr), gw_ref[...]) + gb_ref[...]
    la_ref[...] = (jnp.minimum(z, 0.0) - jnp.log(1.0 + jnp.exp(-jnp.abs(z)))) * (1.0 / GATE_TAU)


def _in_proj(h, lw, tabs, consts, seq, tm):
    n = h.shape[0]
    nt = n // tm
    spb = seq // tm

    def row(i):
        return (i, 0)

    def full(i):
        return (0, 0)

    def tab(i):
        return (i % spb, 0)

    f32 = jnp.float32
    bf16 = jnp.bfloat16
    out_shape = (
        jax.ShapeDtypeStruct((n, 256), bf16),
        jax.ShapeDtypeStruct((n, 256), bf16),
        jax.ShapeDtypeStruct((n, 512), bf16),
        jax.ShapeDtypeStruct((n, 128), bf16),
        jax.ShapeDtypeStruct((VT_ROWS, n), bf16),
        jax.ShapeDtypeStruct((n, 256), f32),
        jax.ShapeDtypeStruct((n, 256), f32),
        jax.ShapeDtypeStruct((n, 256), f32),
        jax.ShapeDtypeStruct((n, 256), f32),
    )
    in_specs = [
        pl.BlockSpec((tm, D_MODEL), row),
        pl.BlockSpec((1, D_MODEL), full),
        pl.BlockSpec((D_MODEL, MAIN_COLS), full),
        pl.BlockSpec((V_ROWS, D_MODEL), full),
        pl.BlockSpec((1, QK_COLS), full),
        pl.BlockSpec((tm, 128), tab),
        pl.BlockSpec((tm, 128), tab),
        pl.BlockSpec((tm, 128), tab),
        pl.BlockSpec((tm, 128), tab),
        pl.BlockSpec((256, 256), full),
        pl.BlockSpec((256, 256), full),
        pl.BlockSpec((256, 256), full),
        pl.BlockSpec((128, 256), full),
        pl.BlockSpec((1, 256), full),
    ]
    out_specs = (
        pl.BlockSpec((tm, 256), row),
        pl.BlockSpec((tm, 256), row),
        pl.BlockSpec((tm, 512), row),
        pl.BlockSpec((tm, 128), row),
        pl.BlockSpec((VT_ROWS, tm), lambda i: (0, i)),
        pl.BlockSpec((tm, 256), row),
        pl.BlockSpec((tm, 256), row),
        pl.BlockSpec((tm, 256), row),
        pl.BlockSpec((tm, 256), row),
    )
    return pl.pallas_call(
        _in_proj_kernel, out_shape=out_shape, grid=(nt,), in_specs=in_specs, out_specs=out_specs,
        compiler_params=_cparams(("parallel",)), name="in_proj",
    )(h, lw["g_mix"], lw["w_main"], lw["w_vt"], lw["gn"], tabs["cosa"], tabs["sina"], tabs["cosb"],
      tabs["sinb"], consts["m32"], consts["m64"], consts["rot"], lw["gw"], lw["gb"])


def _softmax_pv(k, qm, vt, bounded):
    st = _dot_nt(k, qm)
    if bounded:
        p = jnp.exp2(st)
    else:
        p = jnp.exp2(st - jnp.max(st, axis=0, keepdims=True))
    ot = _dot(vt, _bf(p))
    return ot[0:HEAD_DIM, :] / ot[HEAD_DIM:HEAD_DIM + 1, :]


def _diff_attn_kernel(q_ref, k_ref, vt_ref, lam_ref, gsub_ref, o_ref, *, bounded):
    q = q_ref[...]
    k = k_ref[...]
    lane = _iota(q.shape, 1)
    zero = jnp.zeros_like(q)
    lam = lam_ref[...]
    gsub = gsub_ref[...]
    outs = []
    for h in range(A_HEADS):
        vt = vt_ref[VT_HEAD_ROWS * h:VT_HEAD_ROWS * (h + 1), :]
        comp = []
        for c in range(2):
            lo = A_DH * (2 * h + c)
            qm = jnp.where((lane >= lo) & (lane < lo + A_DH), q, zero)
            comp.append(_softmax_pv(k, qm, vt, bounded))
        od = comp[0] - lam * comp[1]
        r = lax.rsqrt(jnp.mean(od * od, axis=0, keepdims=True) + EPS)
        outs.append(od * r * gsub)
    ot = jnp.concatenate(outs, axis=0)
    o_ref[...] = _bf(ot.T)


def _gqa_attn_kernel(q_ref, k_ref, vt_ref, o_ref, *, bounded):
    k = k_ref[...]
    outs = []
    for j in range(B_HEADS // B_KV_HEADS):
        q = q_ref[:, 128 * j:128 * (j + 1)]
        lane = _iota(q.shape, 1)
        zero = jnp.zeros_like(q)
        for g in range(B_KV_HEADS):
            qm = jnp.where((lane >= HEAD_DIM * g) & (lane < HEAD_DIM * (g + 1)), q, zero)
            vt = vt_ref[VT_HEAD_ROWS * g:VT_HEAD_ROWS * (g + 1), :]
            outs.append(_softmax_pv(k, qm, vt, bounded))
    ot = jnp.concatenate(outs, axis=0)
    o_ref[...] = _bf(ot.T)


def _diff_attn(qa, ka, vt, lam, gsub, batch, seq, bounded):
    n = qa.shape[0]
    tq = _tile(TQ_BOUNDED if bounded else TQ_SHIFTED, seq)
    nq = seq // tq
    return pl.pallas_call(
        functools.partial(_diff_attn_kernel, bounded=bounded),
        out_shape=jax.ShapeDtypeStruct((n, 256), jnp.bfloat16),
        grid=(batch, nq),
        in_specs=[
            pl.BlockSpec((tq, 256), lambda b, i: (b * nq + i, 0)),
            pl.BlockSpec((seq, 256), lambda b, i: (b, 0)),
            pl.BlockSpec((A_HEADS * VT_HEAD_ROWS, seq), lambda b, i: (0, b)),
            pl.BlockSpec((1, 1), lambda b, i: (0, 0)),
            pl.BlockSpec((HEAD_DIM, 1), lambda b, i: (0, 0)),
        ],
        out_specs=pl.BlockSpec((tq, 256), lambda b, i: (b * nq + i, 0)),
        compiler_params=_cparams(("parallel", "parallel")),
        name="diff_attn" if bounded else "diff_attn_shifted",
    )(qa, ka, vt, lam, gsub)


def _gqa_attn(qb, kb, vt, batch, seq, bounded):
    n = qb.shape[0]
    tq = _tile(TQ_BOUNDED if bounded else TQ_SHIFTED, seq)
    nq = seq // tq
    return pl.pallas_call(
        functools.partial(_gqa_attn_kernel, bounded=bounded),
        out_shape=jax.ShapeDtypeStruct((n, 512), jnp.bfloat16),
        grid=(batch, nq),
        in_specs=[
            pl.BlockSpec((tq, 512), lambda b, i: (b * nq + i, 0)),
            pl.BlockSpec((seq, 128), lambda b, i: (b, 0)),
            pl.BlockSpec((B_KV_HEADS * VT_HEAD_ROWS, seq), lambda b, i: (A_HEADS // B_KV_HEADS, b)),
        ],
        out_specs=pl.BlockSpec((tq, 512), lambda b, i: (b * nq + i, 0)),
        compiler_params=_cparams(("parallel", "parallel")),
        name="gqa_attn" if bounded else "gqa_attn_shifted",
    )(qb, kb, vt)


def _split3(x):
    hi = _bf(x)
    r1 = x - hi.astype(jnp.float32)
    mid = _bf(r1)
    lo = _bf(r1 - mid.astype(jnp.float32))
    return hi, mid, lo


def _gla_kernel(qk_ref, g_ref, v_ref, o_ref, st_ref, *, nchunk, rev):
    C = GLA_CHUNK
    SB = GLA_SUB
    nsub = C // SB

    @pl.when(pl.program_id(1) == 0)
    def _():
        st_ref[...] = jnp.zeros_like(st_ref)

    r64 = _iota((C, C), 0)
    c64 = _iota((C, C), 1)
    tri = _bf(jnp.where((c64 >= r64) if rev else (c64 <= r64), 1.0, 0.0))
    hm_k = (_iota((4 * C, 128), 0) >> 6) == (_iota((4 * C, 128), 1) >> 5)
    hm_v = (_iota((4 * C, 256), 0) >> 6) == (_iota((4 * C, 256), 1) >> 6)
    ones_exp = _bf(jnp.where((_iota((128, 256), 0) >> 5) == (_iota((128, 256), 1) >> 6), 1.0, 0.0))
    rowc = _iota((C, 128), 0)
    rows = _iota((SB, 128), 0)

    order = range(nchunk - 1, -1, -1) if rev else range(nchunk)
    for ci in order:
        r0 = ci * C
        q = qk_ref[r0:r0 + C, 0:128]
        k = qk_ref[r0:r0 + C, 128:256]
        g = g_ref[r0:r0 + C, :]
        v = v_ref[r0:r0 + C, :]
        g1, g2, g3 = _split3(g)
        b = _dot(tri, g1) + _dot(tri, g2) + _dot(tri, g3)

        st = st_ref[...]
        o = _dot_nt(_bf(q * jnp.exp(b)), _bf(st))
        bl = b[0:1, :] if rev else b[C - 1:C, :]
        ke = k * jnp.exp(bl - b)
        dst = _dot_tn(_bf(v), _bf(ke))
        st_ref[...] = st * jnp.exp(bl) + jnp.where(hm_k, dst, 0.0)

        att = jnp.zeros((C, 4 * C), jnp.float32)
        for blk in (range(0, nsub - 1) if rev else range(1, nsub)):
            if rev:
                ref = b[SB * blk + SB - 1:SB * blk + SB, :]
                kmask = rowc >= SB * (blk + 1)
            else:
                ref = b[SB * blk:SB * blk + 1, :]
                kmask = rowc < SB * blk
            qmask = (rowc >= SB * blk) & (rowc < SB * (blk + 1))
            qd = q * jnp.exp(jnp.where(qmask, b - ref, NEG_INF))
            kd = k * jnp.exp(jnp.where(kmask, ref - b, NEG_INF))
            kexp = jnp.where(hm_k, jnp.concatenate([kd] * 4, axis=0), 0.0)
            att = att + _dot_nt(_bf(qd), _bf(kexp))
        vexp = jnp.where(hm_v, jnp.concatenate([v] * 4, axis=0), 0.0)
        o = o + _dot(_bf(att), _bf(vexp))

        diag = []
        for blk in range(nsub):
            s0 = SB * blk
            bs = b[s0:s0 + SB, :]
            qs = q[s0:s0 + SB, :]
            ks = k[s0:s0 + SB, :]
            vs = v[s0:s0 + SB, :]
            ws = []
            for j in range(SB):
                keep = (rows <= j) if rev else (rows >= j)
                e = jnp.exp(jnp.where(keep, bs - bs[j:j + 1, :], NEG_INF))
                ws.append(e * qs * ks[j:j + 1, :])
            p = _dot(_bf(jnp.concatenate(ws, axis=0)), ones_exp)
            od = p[0:SB, :] * vs[0:1, :]
            for j in range(1, SB):
                od = od + p[SB * j:SB * (j + 1), :] * vs[j:j + 1, :]
            diag.append(od)
        o_ref[r0:r0 + C, :] = o + jnp.concatenate(diag, axis=0)


def _gla(cqk, la, cv, batch, seq, tb, rev):
    n = cqk.shape[0]
    nb = seq // tb

    def blk(b, i):
        return (b * nb + (nb - 1 - i if rev else i), 0)

    def gblk(b, i):
        return (b * nb + (nb - 1 - i if rev else i), 1 if rev else 0)

    return pl.pallas_call(
        functools.partial(_gla_kernel, nchunk=tb // GLA_CHUNK, rev=rev),
        out_shape=jax.ShapeDtypeStruct((n, 256), jnp.float32),
        grid=(batch, nb),
        in_specs=[
            pl.BlockSpec((tb, 256), blk),
            pl.BlockSpec((tb, 128), gblk),
            pl.BlockSpec((tb, 256), blk),
        ],
        out_specs=pl.BlockSpec((tb, 256), blk),
        scratch_shapes=[pltpu.VMEM((256, 128), jnp.float32)],
        compiler_params=_cparams(("parallel", "arbitrary")),
        name="gla_bwd" if rev else "gla_fwd",
    )(cqk, la, cv)


def _out_proj_kernel(h_ref, oa_ref, ob_ref, of_ref, obw_ref, cg_ref, gc_ref, m64_ref, wo_ref, gffn_ref,
                     wrh_ref, wrl_ref, h1_ref, xn_ref, route_ref, cnt_ref):
    @pl.when(pl.program_id(0) == 0)
    def _():
        cnt_ref[...] = jnp.zeros_like(cnt_ref)

    oc = of_ref[...] + obw_ref[...]
    ms = _dot(_bf(oc * oc), m64_ref[...])
    oc = oc * lax.rsqrt(ms + EPS) * gc_ref[...] * cg_ref[...]
    h1 = (h_ref[...] + _dot(oa_ref[...], wo_ref[0:256, :]) + _dot(ob_ref[...], wo_ref[256:768, :])
          + _dot(_bf(oc), wo_ref[768:1024, :]))
    h1_ref[...] = h1
    xn = h1 * lax.rsqrt(jnp.mean(h1 * h1, axis=-1, keepdims=True) + EPS) * gffn_ref[...]
    xn_ref[...] = xn
    xh = _bf(xn)
    xl = _bf(xn - xh.astype(jnp.float32))
    wrh = wrh_ref[...]
    lg = _dot(xh, wrh) + _dot(xh, wrl_ref[...]) + _dot(xl, wrh)

    lane = _iota(lg.shape, 1)
    big = jnp.int32(1 << 20)
    is_g = lane < N_GROUPS
    mg = jnp.max(jnp.where(is_g, lg, NEG_INF), axis=-1, keepdims=True)
    sg = jnp.sum(jnp.where(is_g, jnp.exp(lg - mg), 0.0), axis=-1, keepdims=True)
    p_top = 1.0 / sg
    g_top = jnp.min(jnp.where(is_g & (lg == mg), lane, big), axis=-1, keepdims=True)
    e_lo = ROUTE_OFF + EXPERTS_PER_GROUP * g_top
    sel = (lane >= e_lo) & (lane < e_lo + EXPERTS_PER_GROUP)
    ev0 = jnp.max(jnp.where(sel, lg, NEG_INF), axis=-1, keepdims=True)
    i0 = jnp.min(jnp.where(sel & (lg == ev0), lane, big), axis=-1, keepdims=True)
    sel1 = sel & (lane != i0)
    ev1 = jnp.max(jnp.where(sel1, lg, NEG_INF), axis=-1, keepdims=True)
    i1 = jnp.min(jnp.where(sel1 & (lg == ev1), lane, big), axis=-1, keepdims=True)
    t = jnp.exp(ev1 - ev0)
    pe0 = p_top / (1.0 + t)
    pe1 = p_top * t / (1.0 + t)

    tm = lg.shape[0]
    picked = (lane == i0) | (lane == i1)
    strict_lower = _bf(jnp.where(_iota((tm, tm), 1) < _iota((tm, tm), 0), 1.0, 0.0))
    prefix = _dot(strict_lower, _bf(jnp.where(picked, 1.0, 0.0))) + cnt_ref[...]
    rank0 = jnp.sum(jnp.where(lane == i0, prefix, 0.0), axis=-1, keepdims=True)
    rank1 = jnp.sum(jnp.where(lane == i1, prefix, 0.0), axis=-1, keepdims=True)
    cnt_ref[...] += jnp.sum(jnp.where(picked, 1.0, 0.0), axis=0, keepdims=True)

    fields = ((i0 - ROUTE_OFF).astype(jnp.float32), (i1 - ROUTE_OFF).astype(jnp.float32), pe0, pe1,
              rank0, rank1)
    rec = jnp.zeros(lg.shape, jnp.float32)
    for f, val in enumerate(fields):
        rec = jnp.where(lane == f, val, rec)
    route_ref[...] = rec


def _out_proj(h, oa, ob, o_f, o_b, cgate, lw, consts, tm):
    n = h.shape[0]

    def row(i):
        return (i, 0)

    def full(i):
        return (0, 0)

    return pl.pallas_call(
        _out_proj_kernel,
        out_shape=(jax.ShapeDtypeStruct((n, D_MODEL), jnp.float32),
                   jax.ShapeDtypeStruct((n, D_MODEL), jnp.float32),
                   jax.ShapeDtypeStruct((n, ROUTE_LANES), jnp.float32),
                   jax.ShapeDtypeStruct((1, ROUTE_LANES), jnp.float32)),
        grid=(n // tm,),
        in_specs=[
            pl.BlockSpec((tm, D_MODEL), row),
            pl.BlockSpec((tm, 256), row),
            pl.BlockSpec((tm, 512), row),
            pl.BlockSpec((tm, 256), row),
            pl.BlockSpec((tm, 256), row),
            pl.BlockSpec((tm, 256), row),
            pl.BlockSpec((1, 256), full),
            pl.BlockSpec((256, 256), full),
            pl.BlockSpec((D_MODEL, D_MODEL), full),
            pl.BlockSpec((1, D_MODEL), full),
            pl.BlockSpec((D_MODEL, ROUTE_LANES), full),
            pl.BlockSpec((D_MODEL, ROUTE_LANES), full),
        ],
        out_specs=(pl.BlockSpec((tm, D_MODEL), row), pl.BlockSpec((tm, D_MODEL), row),
                   pl.BlockSpec((tm, ROUTE_LANES), row), pl.BlockSpec((1, ROUTE_LANES), full)),
        compiler_params=_cparams(("arbitrary",)), name="out_proj",
    )(h, oa, ob, o_f, o_b, cgate, lw["gc"], consts["m64"], lw["w_out"], lw["g_ffn"], lw["wr_hi"],
      lw["wr_lo"])


def _row_copy(src, src_row, dst, dst_row, sem):
    return pltpu.make_async_copy(src.at[pl.ds(src_row, 1), :], dst.at[pl.ds(dst_row, 1), :], sem)


def _scatter_kernel(pos_ref, pad_ref, x_ref, xs_ref, pos_smem, pad_smem, zrow_ref, sem_idx, sem_rows):
    ts = x_ref.shape[0]
    cp = pltpu.make_async_copy(pos_ref, pos_smem, sem_idx)
    cp.start()
    cp.wait()

    def issue(r, carry):
        _row_copy(x_ref, r, xs_ref, pos_smem[0, 0, 2 * r], sem_rows).start(priority=0)
        _row_copy(x_ref, r, xs_ref, pos_smem[0, 0, 2 * r + 1], sem_rows).start(priority=1)
        return carry

    lax.fori_loop(0, ts, issue, 0, unroll=8)

    def drain(r, carry):
        _row_copy(x_ref, 0, xs_ref, 0, sem_rows).wait()
        _row_copy(x_ref, 0, xs_ref, 0, sem_rows).wait()
        return carry

    lax.fori_loop(0, ts, drain, 0, unroll=8)

    @pl.when(pl.program_id(0) == pl.num_programs(0) - 1)
    def _():
        zrow_ref[...] = jnp.zeros_like(zrow_ref)
        cz = pltpu.make_async_copy(pad_ref, pad_smem, sem_idx)
        cz.start()
        cz.wait()

        def fill(start_wait):
            for e in range(N_EXPERTS):
                end = pad_smem[0, 2 * e]
                gap = pad_smem[0, 2 * e + 1]
                for j in range(SUBLANES - 1):
                    @pl.when(j < (gap & (SUBLANES - 1)))
                    def _(j=j, end=end, gap=gap):
                        cp = _row_copy(zrow_ref, 0, xs_ref, end - gap + j, sem_rows)
                        if start_wait:
                            cp.start()
                        else:
                            cp.wait()
                filled = 0
                run = ZERO_ROWS
                while run >= SUBLANES:
                    @pl.when((gap & run) != 0)
                    def _(end=end, filled=filled, run=run):
                        start = pl.multiple_of(end - filled - run, run)
                        cp = pltpu.make_async_copy(zrow_ref.at[pl.ds(0, run), :],
                                                   xs_ref.at[pl.ds(start, run), :], sem_rows)
                        if start_wait:
                            cp.start()
                        else:
                            cp.wait()
                    filled = filled + (gap & run)
                    run //= 2
            tail = pad_smem[0, 2 * N_EXPERTS]
            ntail = pad_smem[0, 2 * N_EXPERTS + 1]
            for j in range(2 * N_EXPERTS):
                @pl.when(j < ntail)
                def _(j=j):
                    start = pl.multiple_of(tail + ZERO_ROWS * j, ZERO_ROWS)
                    cp = pltpu.make_async_copy(zrow_ref, xs_ref.at[pl.ds(start, ZERO_ROWS), :], sem_rows)
                    if start_wait:
                        cp.start()
                    else:
                        cp.wait()

        fill(True)
        fill(False)


def _scatter_rows(xn, pos, padpos, p_rows, ts):
    n = xn.shape[0]
    npad = padpos.shape[1]
    return pl.pallas_call(
        _scatter_kernel,
        out_shape=jax.ShapeDtypeStruct((p_rows, D_MODEL), jnp.float32),
        grid=(n // ts,),
        in_specs=[
            pl.BlockSpec((1, 1, 2 * ts), lambda i: (i, 0, 0)),
            pl.BlockSpec((1, npad), lambda i: (0, 0)),
            pl.BlockSpec((ts, D_MODEL), lambda i: (i, 0)),
        ],
        out_specs=pl.BlockSpec(memory_space=pl.ANY),
        scratch_shapes=[
            pltpu.SMEM((1, 1, 2 * ts), jnp.int32),
            pltpu.SMEM((1, npad), jnp.int32),
            pltpu.VMEM((ZERO_ROWS, D_MODEL), jnp.float32),
            pltpu.SemaphoreType.DMA(()),
            pltpu.SemaphoreType.DMA(()),
        ],
        compiler_params=_cparams(("arbitrary",)), name="moe_scatter",
    )(pos, padpos, xn)


def _expert_kernel(te_ref, nu_ref, x_ref, w1_ref, w3_ref, w2_ref, y_ref):
    @pl.when(pl.program_id(0) < nu_ref[0])
    def _():
        x = _bf(x_ref[...])
        a = _dot(x, w1_ref[...])
        hid = a * (1.0 / (1.0 + jnp.exp(-a))) * _dot(x, w3_ref[...])
        y_ref[...] = _dot(_bf(hid), w2_ref[...])

    @pl.when(pl.program_id(0) >= nu_ref[0])
    def _():
        y_ref[...] = jnp.zeros_like(y_ref)


def _experts(xs, tile_expert, n_used, lw, te):
    p_rows = xs.shape[0]

    def row(t, te_ref, nu_ref):
        return (t, 0)

    def wexp(t, te_ref, nu_ref):
        return (te_ref[t], 0, 0)

    grid_spec = pltpu.PrefetchScalarGridSpec(
        num_scalar_prefetch=2,
        grid=(p_rows // te,),
        in_specs=[
            pl.BlockSpec((te, D_MODEL), row),
            pl.BlockSpec((None, D_MODEL, D_EXPERT), wexp),
            pl.BlockSpec((None, D_MODEL, D_EXPERT), wexp),
            pl.BlockSpec((None, D_EXPERT, D_MODEL), wexp),
        ],
        out_specs=pl.BlockSpec((te, D_MODEL), row),
    )
    return pl.pallas_call(
        _expert_kernel, grid_spec=grid_spec,
        out_shape=jax.ShapeDtypeStruct((p_rows, D_MODEL), jnp.float32),
        compiler_params=_cparams(("arbitrary",)), name="moe_experts",
    )(tile_expert, n_used, xs, lw["w1"], lw["w3"], lw["w2"])


def _combine_ple_kernel(pos_ref, posn_ref, ys_ref, h1_ref, route_ref, p_ref, gple_ref, wpg_ref, wpp_ref,
                        o_ref, pos_smem, buf_ref, sem_idx, sem_rows):
    i = pl.program_id(0)
    nsteps = pl.num_programs(0)
    tc = h1_ref.shape[0]
    slot = lax.rem(i, 2)

    def issue_tile(src_pos_ref, s):
        cp = pltpu.make_async_copy(src_pos_ref, pos_smem, sem_idx)
        cp.start()
        cp.wait()

        def issue(r, carry):
            _row_copy(ys_ref, pos_smem[0, 0, 2 * r], buf_ref.at[s, 0], r, sem_rows.at[s]).start(priority=0)
            _row_copy(ys_ref, pos_smem[0, 0, 2 * r + 1], buf_ref.at[s, 1], r, sem_rows.at[s]).start(priority=1)
            return carry

        lax.fori_loop(0, tc, issue, 0, unroll=8)

    @pl.when(i == 0)
    def _():
        issue_tile(pos_ref, 0)

    @pl.when(i + 1 < nsteps)
    def _():
        issue_tile(posn_ref, 1 - slot)

    def drain(r, carry):
        _row_copy(ys_ref, 0, buf_ref.at[slot, 0], 0, sem_rows.at[slot]).wait()
        _row_copy(ys_ref, 0, buf_ref.at[slot, 1], 0, sem_rows.at[slot]).wait()
        return carry

    lax.fori_loop(0, tc, drain, 0, unroll=8)

    route = route_ref[...]
    lane = _iota(route.shape, 1)
    w0 = jnp.sum(jnp.where(lane == 2, route, 0.0), axis=-1, keepdims=True)
    w1 = jnp.sum(jnp.where(lane == 3, route, 0.0), axis=-1, keepdims=True)
    h2 = h1_ref[...] + (w0 * buf_ref[slot, 0] + w1 * buf_ref[slot, 1])
    xn = h2 * lax.rsqrt(jnp.mean(h2 * h2, axis=-1, keepdims=True) + EPS) * gple_ref[...]
    gate = 1.0 / (1.0 + jnp.exp(-_dot(_bf(xn), wpg_ref[...])))
    o_ref[...] = h2 + _dot(_bf(p_ref[...]), wpp_ref[...]) * gate


def _combine_ple(ys, pos, h1, route, p_l, lw, tc):
    n = h1.shape[0]
    nsteps = n // tc

    def row(i):
        return (i, 0)

    def full(i):
        return (0, 0)

    return pl.pallas_call(
        _combine_ple_kernel,
        out_shape=jax.ShapeDtypeStruct((n, D_MODEL), jnp.float32),
        grid=(nsteps,),
        in_specs=[
            pl.BlockSpec((1, 1, 2 * tc), lambda i: (i, 0, 0)),
            pl.BlockSpec((1, 1, 2 * tc), lambda i: (jnp.minimum(i + 1, nsteps - 1), 0, 0)),
            pl.BlockSpec(memory_space=pl.ANY),
            pl.BlockSpec((tc, D_MODEL), row),
            pl.BlockSpec((tc, ROUTE_LANES), row),
            pl.BlockSpec((tc, PLE_DIM), row),
            pl.BlockSpec((1, D_MODEL), full),
            pl.BlockSpec((D_MODEL, D_MODEL), full),
            pl.BlockSpec((PLE_DIM, D_MODEL), full),
        ],
        out_specs=pl.BlockSpec((tc, D_MODEL), row),
        scratch_shapes=[
            pltpu.SMEM((1, 1, 2 * tc), jnp.int32),
            pltpu.VMEM((2, 2, tc, D_MODEL), jnp.float32),
            pltpu.SemaphoreType.DMA(()),
            pltpu.SemaphoreType.DMA((2,)),
        ],
        compiler_params=_cparams(("arbitrary",)), name="moe_combine_ple",
    )(pos, pos, ys, h1, route, p_l, lw["g_ple"], lw["w_pg"], lw["w_pp"])


def _moe_ple(h1, xn, route, counts, p_l, lw, ts, te):
    n = h1.shape[0]
    p_rows = 2 * n + N_EXPERTS * te
    i32 = jnp.int32
    cnt = counts[0, ROUTE_OFF:ROUTE_OFF + N_EXPERTS].astype(i32)
    padded = ((cnt + te - 1) // te) * te
    pend = jnp.cumsum(padded)
    pstart = pend - padded
    e01 = route[:, 0:2].astype(i32)
    rank01 = route[:, 4:6].astype(i32)
    pos = (pstart[e01] + rank01).reshape(n // ts, 1, 2 * ts)
    n_tiles = p_rows // te
    tile_expert = jnp.minimum(
        jnp.searchsorted(pend, jnp.arange(n_tiles, dtype=i32) * te, side="right"), N_EXPERTS - 1).astype(i32)
    n_used = (pend[-1] // te).astype(i32).reshape(1)
    assert te == 2 * ZERO_ROWS
    padpos = jnp.stack([pend, padded - cnt], axis=1).reshape(2 * N_EXPERTS)
    tail = jnp.stack([pend[-1], (p_rows - pend[-1]) // ZERO_ROWS])
    padpos = jnp.pad(jnp.concatenate([padpos, tail]), (0, 128 - 2 * N_EXPERTS - 2)).astype(i32)[None, :]

    xs = _scatter_rows(xn, pos, padpos, p_rows, ts)
    ys = _experts(xs, tile_expert, n_used, lw, te)
    return _combine_ple(ys, pos, h1, route, p_l, lw, ts)


_B_HEAD_ORDER = np.array([0, 4, 1, 5, 2, 6, 3, 7])


def _block_mean_matrix(width, block):
    idx = np.arange(width)
    return ((idx[:, None] // block) == (idx[None, :] // block)).astype(np.float32) / block


def _rotate_half_matrix(width, block):
    half = block // 2
    m = np.zeros((width, width), np.float32)
    for j in range(width):
        if j % block < half:
            m[j + half, j] = -1.0
        else:
            m[j - half, j] = 1.0
    return m


def _constants():
    bf16 = jnp.bfloat16
    return {
        "m32": jnp.asarray(_block_mean_matrix(256, A_DH), bf16),
        "m64": jnp.asarray(_block_mean_matrix(256, HEAD_DIM), bf16),
        "rot": jnp.asarray(_rotate_half_matrix(256, A_DH), bf16),
    }


def _rope_tables(seq):
    def table(pos, dim):
        inv = 1.0 / (ROPE_THETA ** (jnp.arange(0, dim, 2, dtype=jnp.float32) / dim))
        ang = pos.astype(jnp.float32)[:, None] * inv[None, :]
        return jnp.cos(ang), jnp.sin(ang)

    rows = seq // GRID_W
    pos = jnp.arange(seq)
    row = jnp.repeat(jnp.arange(rows), GRID_W)
    col = jnp.tile(jnp.arange(GRID_W), rows)
    c1, s1 = table(pos, A_DH)
    cr, sr = table(row, HEAD_DIM // 2)
    cc, sc = table(col, HEAD_DIM // 2)
    return {
        "cosa": jnp.tile(jnp.concatenate([c1, c1], axis=1), (1, 4)),
        "sina": jnp.tile(jnp.concatenate([s1, s1], axis=1), (1, 4)),
        "cosb": jnp.tile(jnp.concatenate([cr, cr, cc, cc], axis=1), (1, 2)),
        "sinb": jnp.tile(jnp.concatenate([sr, sr, sc, sc], axis=1), (1, 2)),
    }


def _prep_layer(l, g_mix, w_in, a_q_norm, a_k_norm, a_lambda, a_subln, b_q_norm, b_k_norm, c_gate_w,
                c_gate_b, c_out_norm, w_out, g_ffn, w_rg, w_re, w_e1, w_e3, w_e2, g_ple, w_pg, w_pp):
    f32 = jnp.float32
    bf16 = jnp.bfloat16
    aq, ak, av = w_in[:, 0:256], w_in[:, 256:512], w_in[:, 512:768]
    bq, bk, bv = w_in[:, 768:1280], w_in[:, 1280:1408], w_in[:, 1408:1536]
    crest = w_in[:, 1536:2336]
    bq = bq.reshape(D_MODEL, B_HEADS, HEAD_DIM)[:, _B_HEAD_ORDER, :].reshape(D_MODEL, 512)
    w_main = jnp.concatenate(
        [aq, ak, bq, bk, crest, jnp.zeros((D_MODEL, MAIN_COLS - QK_COLS - 800), f32)], axis=1)
    w_vt = jnp.concatenate([av, bv], axis=1).T

    gn = jnp.concatenate([
        jnp.tile(a_q_norm, 8) * (A_DH ** -0.5 * LOG2E),
        jnp.tile(a_k_norm, 8),
        jnp.tile(b_q_norm, 8) * (HEAD_DIM ** -0.5 * LOG2E),
        jnp.tile(b_k_norm, 2),
    ])[None, :]

    gw = jnp.zeros((128, 256), f32)
    gw = gw.at[0:C_GATE_RANK, 0:128].set(c_gate_w[0])
    gw = gw.at[C_GATE_RANK:2 * C_GATE_RANK, 128:256].set(c_gate_w[1])
    gb = jnp.concatenate([c_gate_b[0], c_gate_b[1]])[None, :]

    gq_a = jnp.max(jnp.abs(a_q_norm)) * (A_DH ** -0.5 * LOG2E)
    gq_b = jnp.max(jnp.abs(b_q_norm)) * (HEAD_DIM ** -0.5 * LOG2E)
    bound = 1.05 * jnp.maximum(A_DH * gq_a * jnp.max(jnp.abs(a_k_norm)),
                               HEAD_DIM * gq_b * jnp.max(jnp.abs(b_k_norm)))
    bounded = bound <= SCORE_BOUND

    lam_init = 0.8 - 0.6 * math.exp(-0.3 * l)
    lf = a_lambda.astype(f32)
    lam = jnp.exp(jnp.sum(lf[0] * lf[1])) - jnp.exp(jnp.sum(lf[2] * lf[3])) + lam_init

    wo_b = w_out[256:768].reshape(B_HEADS, HEAD_DIM, D_MODEL)[_B_HEAD_ORDER].reshape(512, D_MODEL)
    w_out_p = jnp.concatenate([w_out[0:256], wo_b, w_out[768:1024]], axis=0)

    wr = jnp.zeros((D_MODEL, ROUTE_LANES), f32)
    wr = wr.at[:, 0:N_GROUPS].set(w_rg)
    wr = wr.at[:, ROUTE_OFF:ROUTE_OFF + N_EXPERTS].set(
        jnp.transpose(w_re, (1, 0, 2)).reshape(D_MODEL, N_EXPERTS))
    wr_hi = wr.astype(bf16)
    wr_lo = (wr - wr_hi.astype(f32)).astype(bf16)

    return {
        "g_mix": g_mix[None, :], "w_main": w_main.astype(bf16), "w_vt": w_vt.astype(bf16), "gn": gn,
        "gw": gw.astype(bf16), "gb": gb, "lam": jnp.reshape(lam, (1, 1)), "bounded": bounded,
        "gsub": (a_subln * (1.0 - lam_init))[:, None],
        "gc": jnp.tile(c_out_norm, C_HEADS)[None, :], "w_out": w_out_p.astype(bf16),
        "g_ffn": g_ffn[None, :], "wr_hi": wr_hi, "wr_lo": wr_lo,
        "w1": w_e1.astype(bf16), "w3": w_e3.astype(bf16), "w2": w_e2.astype(bf16),
        "g_ple": g_ple[None, :], "w_pg": w_pg.astype(bf16), "w_pp": w_pp.astype(bf16),
    }


def _tile(limit, total):
    t = min(limit, total)
    assert total % t == 0
    return t


def _trunk(x, p, layers, consts):
    batch, seq, _ = x.shape
    n = batch * seq
    tabs = _rope_tables(seq)
    tm = _tile(512, seq)
    tb = _tile(256, seq)
    ts = _tile(512, seq)
    te = 512
    h = x.reshape(n, D_MODEL)
    for l in range(DEPTH):
        lw = layers[l]
        qa, ka, qb, kb, vt, cqk, la, cv, cgate = _in_proj(h, lw, tabs, consts, seq, tm)
        oa = lax.cond(lw["bounded"],
                      lambda *a: _diff_attn(*a, batch, seq, True),
                      lambda *a: _diff_attn(*a, batch, seq, False),
                      qa, ka, vt, lw["lam"], lw["gsub"])
        ob = lax.cond(lw["bounded"],
                      lambda *a: _gqa_attn(*a, batch, seq, True),
                      lambda *a: _gqa_attn(*a, batch, seq, False),
                      qb, kb, vt)
        o_f = _gla(cqk, la, cv, batch, seq, tb, rev=False)
        o_b = _gla(cqk, la, cv, batch, seq, tb, rev=True)
        h1, xn, route, counts = _out_proj(h, oa, ob, o_f, o_b, cgate, lw, consts, tm)
        h = _moe_ple(h1, xn, route, counts, p[l].reshape(n, PLE_DIM), lw, ts, te)
    return h.reshape(batch, seq, D_MODEL)


def kernel(x_prompt, x_sample, p_prompt, p_sample, g_mix, w_in, a_q_norm, a_k_norm, a_lambda, a_subln, b_q_norm, b_k_norm, c_gate_w, c_gate_b, c_out_norm, w_out, g_ffn, w_router_group, w_router_expert, w_exp_gate, w_exp_up, w_exp_down, g_ple, w_ple_gate, w_ple_proj):
    consts = _constants()
    layers = [
        _prep_layer(l, g_mix[l], w_in[l], a_q_norm[l], a_k_norm[l], a_lambda[l], a_subln[l], b_q_norm[l],
                    b_k_norm[l], c_gate_w[l], c_gate_b[l], c_out_norm[l], w_out[l], g_ffn[l],
                    w_router_group[l], w_router_expert[l], w_exp_gate[l], w_exp_up[l], w_exp_down[l],
                    g_ple[l], w_ple_gate[l], w_ple_proj[l])
        for l in range(DEPTH)
    ]
    y_prompt = _trunk(x_prompt, p_prompt, layers, consts)
    y_sample = _trunk(x_sample, p_sample, layers, consts)
    return (y_prompt, y_sample)
```

```python
import functools
import math

import jax
import jax.numpy as jnp
import numpy as np
from jax import lax
from jax.experimental import pallas as pl
from jax.experimental.pallas import tpu as pltpu

D_MODEL = 1024
DEPTH = 4
HEAD_DIM = 64
A_HEADS = 4
A_DH = 32
B_HEADS = 8
B_KV_HEADS = 2
C_HEADS = 4
C_DK = 32
C_DV = 64
C_GATE_RANK = 16
GATE_TAU = 16.0
GLA_CHUNK = 64
GLA_SUB = 16
GRID_W = 64
ROPE_THETA = 10000.0
PLE_DIM = 256
N_GROUPS = 4
EXPERTS_PER_GROUP = 4
N_EXPERTS = 16
D_EXPERT = 512
EPS = 1e-6
LOG2E = 1.4426950408889634

VMEM_LIMIT_BYTES = 52 * 1024 * 1024
NEG_INF = float("-inf")

QK_COLS = 1152
MAIN_COLS = 2048
VT_HEAD_ROWS = 80
V_ROWS = 384
VT_ROWS = 6 * VT_HEAD_ROWS
TQ_BOUNDED = 512
TQ_SHIFTED = 256
SCORE_BOUND = 60.0
GLA_DECAY_BOUND = 80.0
TB_BOUNDED = 512
TB_EXACT = 256
ZERO_ROWS = 256
SUBLANES = 8
ROUTE_LANES = 128
ROUTE_OFF = 4
PAIRS_PER_GROUP = 6
N_CLASSES = N_GROUPS * PAIRS_PER_GROUP
PAIR_A = (0, 0, 0, 1, 1, 2)
PAIR_B = (1, 2, 3, 2, 3, 3)
XROW = D_MODEL + 128


def _cparams(sem):
    return pltpu.CompilerParams(dimension_semantics=sem, vmem_limit_bytes=VMEM_LIMIT_BYTES)


def _bf(x):
    return x.astype(jnp.bfloat16)


def _dot(a, b):
    return jnp.dot(a, b, preferred_element_type=jnp.float32)


def _dot_nt(a, b):
    return lax.dot_general(a, b, (((1,), (1,)), ((), ())), preferred_element_type=jnp.float32)


def _dot_tn(a, b):
    return lax.dot_general(a, b, (((0,), (0,)), ((), ())), preferred_element_type=jnp.float32)


def _iota(shape, dim):
    return lax.broadcasted_iota(jnp.int32, shape, dim)


def _in_proj_kernel(h_ref, gmix_ref, w_ref, wvt_ref, gn_ref, cosa_ref, sina_ref, cosb_ref, sinb_ref,
                    m32_ref, m64_ref, rot_ref, gw_ref, gb_ref,
                    qa_ref, ka_ref, qb_ref, kb_ref, vt_ref, cqk_ref, la_ref, cv_ref, cg_ref):
    x = h_ref[...]
    a = x * lax.rsqrt(jnp.mean(x * x, axis=-1, keepdims=True) + EPS) * gmix_ref[...]
    ab = _bf(a)
    proj = _dot(ab, w_ref[...])
    vt = _bf(_dot_nt(wvt_ref[...], ab))
    ones = jnp.ones((VT_HEAD_ROWS - HEAD_DIM, vt.shape[1]), jnp.bfloat16)
    for hh in range(VT_ROWS // VT_HEAD_ROWS):
        vt_ref[VT_HEAD_ROWS * hh:VT_HEAD_ROWS * hh + HEAD_DIM, :] = vt[HEAD_DIM * hh:HEAD_DIM * (hh + 1), :]
        vt_ref[VT_HEAD_ROWS * hh + HEAD_DIM:VT_HEAD_ROWS * (hh + 1), :] = ones

    cosa = cosa_ref[...]
    sina = sina_ref[...]
    cosb = cosb_ref[...]
    sinb = sinb_ref[...]
    cosa2 = jnp.concatenate([cosa, cosa], axis=1)
    sina2 = jnp.concatenate([sina, sina], axis=1)
    cosb2 = jnp.concatenate([cosb, cosb], axis=1)
    sinb2 = jnp.concatenate([sinb, sinb], axis=1)

    def qk_slab(off, width, mean_mat, cos, sin, out_ref, out_off):
        xs = proj[:, off:off + width]
        ms = _dot(_bf(xs * xs), mean_mat)
        xn = xs * lax.rsqrt(ms + EPS) * gn_ref[:, off:off + width]
        rot = _dot(_bf(xn), rot_ref[0:width, 0:width])
        out_ref[:, out_off:out_off + width] = _bf(xn * cos + rot * sin)

    m32 = m32_ref[...]
    m64 = m64_ref[...]
    qk_slab(0, 256, m32, cosa2, sina2, qa_ref, 0)
    qk_slab(256, 256, m32, cosa2, sina2, ka_ref, 0)
    qk_slab(512, 256, m64, cosb2, sinb2, qb_ref, 0)
    qk_slab(768, 256, m64, cosb2, sinb2, qb_ref, 256)
    qk_slab(1024, 128, m64_ref[0:128, 0:128], cosb, sinb, kb_ref, 0)

    c0 = QK_COLS
    cqk_ref[:, 0:128] = proj[:, c0:c0 + 128] * (C_DK ** -0.5)
    cqk_ref[:, 128:256] = proj[:, c0 + 128:c0 + 256]
    cv_ref[...] = proj[:, c0 + 256:c0 + 512]
    cg = proj[:, c0 + 512:c0 + 768]
    cg_ref[...] = cg * (1.0 / (1.0 + jnp.exp(-cg)))
    clr = proj[:, c0 + 768:c0 + 896]
    z = _dot(_bf(clr), gw_ref[...]) + gb_ref[...]
    la_ref[...] = (jnp.minimum(z, 0.0) - jnp.log(1.0 + jnp.exp(-jnp.abs(z)))) * (1.0 / GATE_TAU)


def _in_proj(h, lw, tabs, consts, seq, tm):
    n = h.shape[0]
    nt = n // tm
    spb = seq // tm

    def row(i):
        return (i, 0)

    def full(i):
        return (0, 0)

    def tab(i):
        return (i % spb, 0)

    f32 = jnp.float32
    bf16 = jnp.bfloat16
    out_shape = (
        jax.ShapeDtypeStruct((n, 256), bf16),
        jax.ShapeDtypeStruct((n, 256), bf16),
        jax.ShapeDtypeStruct((n, 512), bf16),
        jax.ShapeDtypeStruct((n, 128), bf16),
        jax.ShapeDtypeStruct((VT_ROWS, n), bf16),
        jax.ShapeDtypeStruct((n, 256), f32),
        jax.ShapeDtypeStruct((n, 256), f32),
        jax.ShapeDtypeStruct((n, 256), f32),
        jax.ShapeDtypeStruct((n, 256), f32),
    )
    in_specs = [
        pl.BlockSpec((tm, D_MODEL), row),
        pl.BlockSpec((1, D_MODEL), full),
        pl.BlockSpec((D_MODEL, MAIN_COLS), full),
        pl.BlockSpec((V_ROWS, D_MODEL), full),
        pl.BlockSpec((1, QK_COLS), full),
        pl.BlockSpec((tm, 128), tab),
        pl.BlockSpec((tm, 128), tab),
        pl.BlockSpec((tm, 128), tab),
        pl.BlockSpec((tm, 128), tab),
        pl.BlockSpec((256, 256), full),
        pl.BlockSpec((256, 256), full),
        pl.BlockSpec((256, 256), full),
        pl.BlockSpec((128, 256), full),
        pl.BlockSpec((1, 256), full),
    ]
    out_specs = (
        pl.BlockSpec((tm, 256), row),
        pl.BlockSpec((tm, 256), row),
        pl.BlockSpec((tm, 512), row),
        pl.BlockSpec((tm, 128), row),
        pl.BlockSpec((VT_ROWS, tm), lambda i: (0, i)),
        pl.BlockSpec((tm, 256), row),
        pl.BlockSpec((tm, 256), row),
        pl.BlockSpec((tm, 256), row),
        pl.BlockSpec((tm, 256), row),
    )
    return pl.pallas_call(
        _in_proj_kernel, out_shape=out_shape, grid=(nt,), in_specs=in_specs, out_specs=out_specs,
        compiler_params=_cparams(("parallel",)), name="in_proj",
    )(h, lw["g_mix"], lw["w_main"], lw["w_vt"], lw["gn"], tabs["cosa"], tabs["sina"], tabs["cosb"],
      tabs["sinb"], consts["m32"], consts["m64"], consts["rot"], lw["gw"], lw["gb"])


def _softmax_pv(k, qm, vt, bounded):
    st = _dot_nt(k, qm)
    if bounded:
        p = jnp.exp2(st)
    else:
        p = jnp.exp2(st - jnp.max(st, axis=0, keepdims=True))
    ot = _dot(vt, _bf(p))
    return ot[0:HEAD_DIM, :] / ot[HEAD_DIM:HEAD_DIM + 1, :]


def _diff_attn_kernel(q_ref, k_ref, vt_ref, lam_ref, gsub_ref, o_ref, *, bounded):
    q = q_ref[...]
    k = k_ref[...]
    lane = _iota(q.shape, 1)
    zero = jnp.zeros_like(q)
    lam = lam_ref[...]
    gsub = gsub_ref[...]
    outs = []
    for h in range(A_HEADS):
        vt = vt_ref[VT_HEAD_ROWS * h:VT_HEAD_ROWS * (h + 1), :]
        comp = []
        for c in range(2):
            lo = A_DH * (2 * h + c)
            qm = jnp.where((lane >= lo) & (lane < lo + A_DH), q, zero)
            comp.append(_softmax_pv(k, qm, vt, bounded))
        od = comp[0] - lam * comp[1]
        r = lax.rsqrt(jnp.mean(od * od, axis=0, keepdims=True) + EPS)
        outs.append(od * r * gsub)
    ot = jnp.concatenate(outs, axis=0)
    o_ref[...] = _bf(ot.T)


def _gqa_attn_kernel(q_ref, k_ref, vt_ref, o_ref, *, bounded):
    k = k_ref[...]
    outs = []
    for j in range(B_HEADS // B_KV_HEADS):
        q = q_ref[:, 128 * j:128 * (j + 1)]
        lane = _iota(q.shape, 1)
        zero = jnp.zeros_like(q)
        for g in range(B_KV_HEADS):
            qm = jnp.where((lane >= HEAD_DIM * g) & (lane < HEAD_DIM * (g + 1)), q, zero)
            vt = vt_ref[VT_HEAD_ROWS * g:VT_HEAD_ROWS * (g + 1), :]
            outs.append(_softmax_pv(k, qm, vt, bounded))
    ot = jnp.concatenate(outs, axis=0)
    o_ref[...] = _bf(ot.T)


def _diff_attn(qa, ka, vt, lam, gsub, batch, seq, bounded):
    n = qa.shape[0]
    tq = _tile(TQ_BOUNDED if bounded else TQ_SHIFTED, seq)
    nq = seq // tq
    return pl.pallas_call(
        functools.partial(_diff_attn_kernel, bounded=bounded),
        out_shape=jax.ShapeDtypeStruct((n, 256), jnp.bfloat16),
        grid=(batch, nq),
        in_specs=[
            pl.BlockSpec((tq, 256), lambda b, i: (b * nq + i, 0)),
            pl.BlockSpec((seq, 256), lambda b, i: (b, 0)),
            pl.BlockSpec((A_HEADS * VT_HEAD_ROWS, seq), lambda b, i: (0, b)),
            pl.BlockSpec((1, 1), lambda b, i: (0, 0)),
            pl.BlockSpec((HEAD_DIM, 1), lambda b, i: (0, 0)),
        ],
        out_specs=pl.BlockSpec((tq, 256), lambda b, i: (b * nq + i, 0)),
        compiler_params=_cparams(("parallel", "parallel")),
        name="diff_attn" if bounded else "diff_attn_shifted",
    )(qa, ka, vt, lam, gsub)


def _gqa_attn(qb, kb, vt, batch, seq, bounded):
    n = qb.shape[0]
    tq = _tile(TQ_BOUNDED if bounded else TQ_SHIFTED, seq)
    nq = seq // tq
    return pl.pallas_call(
        functools.partial(_gqa_attn_kernel, bounded=bounded),
        out_shape=jax.ShapeDtypeStruct((n, 512), jnp.bfloat16),
        grid=(batch, nq),
        in_specs=[
            pl.BlockSpec((tq, 512), lambda b, i: (b * nq + i, 0)),
            pl.BlockSpec((seq, 128), lambda b, i: (b, 0)),
            pl.BlockSpec((B_KV_HEADS * VT_HEAD_ROWS, seq), lambda b, i: (A_HEADS // B_KV_HEADS, b)),
        ],
        out_specs=pl.BlockSpec((tq, 512), lambda b, i: (b * nq + i, 0)),
        compiler_params=_cparams(("parallel", "parallel")),
        name="gqa_attn" if bounded else "gqa_attn_shifted",
    )(qb, kb, vt)


def _split3(x):
    hi = _bf(x)
    r1 = x - hi.astype(jnp.float32)
    mid = _bf(r1)
    lo = _bf(r1 - mid.astype(jnp.float32))
    return hi, mid, lo


def _gla_kernel(qk_ref, g_ref, v_ref, o_ref, st_ref, *, nchunk, rev):
    C = GLA_CHUNK
    SB = GLA_SUB
    nsub = C // SB

    @pl.when(pl.program_id(1) == 0)
    def _():
        st_ref[...] = jnp.zeros_like(st_ref)

    r64 = _iota((C, C), 0)
    c64 = _iota((C, C), 1)
    tri = _bf(jnp.where((c64 >= r64) if rev else (c64 <= r64), 1.0, 0.0))
    hm_k = (_iota((4 * C, 128), 0) >> 6) == (_iota((4 * C, 128), 1) >> 5)
    hm_v = (_iota((4 * C, 256), 0) >> 6) == (_iota((4 * C, 256), 1) >> 6)
    ones_exp = _bf(jnp.where((_iota((128, 256), 0) >> 5) == (_iota((128, 256), 1) >> 6), 1.0, 0.0))
    rowc = _iota((C, 128), 0)
    rows = _iota((SB, 128), 0)

    order = range(nchunk - 1, -1, -1) if rev else range(nchunk)
    for ci in order:
        r0 = ci * C
        q = qk_ref[r0:r0 + C, 0:128]
        k = qk_ref[r0:r0 + C, 128:256]
        g = g_ref[r0:r0 + C, :]
        v = v_ref[r0:r0 + C, :]
        g1, g2, g3 = _split3(g)
        b = _dot(tri, g1) + _dot(tri, g2) + _dot(tri, g3)

        st = st_ref[...]
        o = _dot_nt(_bf(q * jnp.exp(b)), _bf(st))
        bl = b[0:1, :] if rev else b[C - 1:C, :]
        ke = k * jnp.exp(bl - b)
        dst = _dot_tn(_bf(v), _bf(ke))
        st_ref[...] = st * jnp.exp(bl) + jnp.where(hm_k, dst, 0.0)

        att = jnp.zeros((C, 4 * C), jnp.float32)
        for blk in (range(0, nsub - 1) if rev else range(1, nsub)):
            if rev:
                ref = b[SB * blk + SB - 1:SB * blk + SB, :]
                kmask = rowc >= SB * (blk + 1)
            else:
                ref = b[SB * blk:SB * blk + 1, :]
                kmask = rowc < SB * blk
            qmask = (rowc >= SB * blk) & (rowc < SB * (blk + 1))
            qd = q * jnp.exp(jnp.where(qmask, b - ref, NEG_INF))
            kd = k * jnp.exp(jnp.where(kmask, ref - b, NEG_INF))
            kexp = jnp.where(hm_k, jnp.concatenate([kd] * 4, axis=0), 0.0)
            att = att + _dot_nt(_bf(qd), _bf(kexp))
        vexp = jnp.where(hm_v, jnp.concatenate([v] * 4, axis=0), 0.0)
        o = o + _dot(_bf(att), _bf(vexp))

        diag = []
        for blk in range(nsub):
            s0 = SB * blk
            bs = b[s0:s0 + SB, :]
            qs = q[s0:s0 + SB, :]
            ks = k[s0:s0 + SB, :]
            vs = v[s0:s0 + SB, :]
            ws = []
            for j in range(SB):
                keep = (rows <= j) if rev else (rows >= j)
                e = jnp.exp(jnp.where(keep, bs - bs[j:j + 1, :], NEG_INF))
                ws.append(e * qs * ks[j:j + 1, :])
            p = _dot(_bf(jnp.concatenate(ws, axis=0)), ones_exp)
            od = p[0:SB, :] * vs[0:1, :]
            for j in range(1, SB):
                od = od + p[SB * j:SB * (j + 1), :] * vs[j:j + 1, :]
            diag.append(od)
        o_ref[r0:r0 + C, :] = o + jnp.concatenate(diag, axis=0)


def _gla_bounded_kernel(qk_ref, g_ref, v_ref, o_ref, st_ref, *, nchunk, rev):
    C = GLA_CHUNK

    @pl.when(pl.program_id(1) == 0)
    def _():
        st_ref[...] = jnp.zeros_like(st_ref)

    r64 = _iota((C, C), 0)
    c64 = _iota((C, C), 1)
    tri = _bf(jnp.where((c64 >= r64) if rev else (c64 <= r64), 1.0, 0.0))
    hm_k = (_iota((4 * C, 128), 0) >> 6) == (_iota((4 * C, 128), 1) >> 5)
    hm_v = (_iota((4 * C, 256), 0) >> 6) == (_iota((4 * C, 256), 1) >> 6)
    qpos = _iota((C, 4 * C), 0)
    kpos = _iota((C, 4 * C), 1) & (C - 1)
    keep = (kpos >= qpos) if rev else (kpos <= qpos)

    order = range(nchunk - 1, -1, -1) if rev else range(nchunk)
    for ci in order:
        r0 = ci * C
        q = qk_ref[r0:r0 + C, 0:128]
        k = qk_ref[r0:r0 + C, 128:256]
        g = g_ref[r0:r0 + C, :]
        v = v_ref[r0:r0 + C, :]
        g1, g2, g3 = _split3(g)
        b = _dot(tri, g1) + _dot(tri, g2) + _dot(tri, g3)
        qe = _bf(q * jnp.exp(b))
        kd = k * jnp.exp(-b)
        bl = b[0:1, :] if rev else b[C - 1:C, :]
        ebl = jnp.exp(bl)

        kexp = jnp.where(hm_k, jnp.concatenate([kd] * 4, axis=0), 0.0)
        att = jnp.where(keep, _dot_nt(qe, _bf(kexp)), 0.0)
        vb = _bf(v)
        vexp = jnp.where(hm_v, jnp.concatenate([vb] * 4, axis=0), jnp.zeros((), jnp.bfloat16))
        st = st_ref[...]
        o_ref[r0:r0 + C, :] = _dot(_bf(att), vexp) + _dot_nt(qe, _bf(st))
        dst = _dot_tn(vb, _bf(kd * ebl))
        st_ref[...] = st * ebl + jnp.where(hm_k, dst, 0.0)


def _gla(cqk, la, cv, batch, seq, rev, bounded):
    n = cqk.shape[0]
    tb = _tile(TB_BOUNDED if bounded else TB_EXACT, seq)
    nb = seq // tb

    def blk(b, i):
        return (b * nb + (nb - 1 - i if rev else i), 0)

    def gblk(b, i):
        return (b * nb + (nb - 1 - i if rev else i), 1 if rev else 0)

    return pl.pallas_call(
        functools.partial(_gla_bounded_kernel if bounded else _gla_kernel, nchunk=tb // GLA_CHUNK, rev=rev),
        out_shape=jax.ShapeDtypeStruct((n, 256), jnp.float32),
        grid=(batch, nb),
        in_specs=[
            pl.BlockSpec((tb, 256), blk),
            pl.BlockSpec((tb, 128), gblk),
            pl.BlockSpec((tb, 256), blk),
        ],
        out_specs=pl.BlockSpec((tb, 256), blk),
        scratch_shapes=[pltpu.VMEM((256, 128), jnp.float32)],
        compiler_params=_cparams(("parallel", "arbitrary")),
        name=("gla_bwd" if rev else "gla_fwd") + ("" if bounded else "_exact"),
    )(cqk, la, cv)


def _out_proj_kernel(h_ref, oa_ref, ob_ref, of_ref, obw_ref, cg_ref, gc_ref, m64_ref, wo_ref, gffn_ref,
                     wrt_ref, h1_ref, xn_ref, route_ref, cnt_ref):
    @pl.when(pl.program_id(0) == 0)
    def _():
        cnt_ref[...] = jnp.zeros_like(cnt_ref)

    oc = of_ref[...] + obw_ref[...]
    ms = _dot(_bf(oc * oc), m64_ref[...])
    oc = oc * lax.rsqrt(ms + EPS) * gc_ref[...] * cg_ref[...]
    h1 = (h_ref[...] + _dot(oa_ref[...], wo_ref[0:256, :]) + _dot(ob_ref[...], wo_ref[256:768, :])
          + _dot(_bf(oc), wo_ref[768:1024, :]))
    h1_ref[...] = h1
    xn = h1 * lax.rsqrt(jnp.mean(h1 * h1, axis=-1, keepdims=True) + EPS) * gffn_ref[...]
    xn_ref[:, 0:D_MODEL] = xn
    xh = _bf(xn)
    xl = _bf(xn - xh.astype(jnp.float32))
    lgt = _dot_nt(wrt_ref[...], xh)
    lgt = lgt[0:ROUTE_LANES, :] + lgt[ROUTE_LANES:2 * ROUTE_LANES, :] + _dot_nt(wrt_ref[0:ROUTE_LANES, :], xl)
    lg = lgt.T

    lane = _iota(lg.shape, 1)
    big = jnp.int32(1 << 20)
    is_g = lane < N_GROUPS
    mg = jnp.max(jnp.where(is_g, lg, NEG_INF), axis=-1, keepdims=True)
    sg = jnp.sum(jnp.where(is_g, jnp.exp(lg - mg), 0.0), axis=-1, keepdims=True)
    p_top = 1.0 / sg
    g_top = jnp.min(jnp.where(is_g & (lg == mg), lane, big), axis=-1, keepdims=True)
    e_lo = ROUTE_OFF + EXPERTS_PER_GROUP * g_top
    sel = (lane >= e_lo) & (lane < e_lo + EXPERTS_PER_GROUP)
    ev0 = jnp.max(jnp.where(sel, lg, NEG_INF), axis=-1, keepdims=True)
    i0 = jnp.min(jnp.where(sel & (lg == ev0), lane, big), axis=-1, keepdims=True)
    sel1 = sel & (lane != i0)
    ev1 = jnp.max(jnp.where(sel1, lg, NEG_INF), axis=-1, keepdims=True)
    i1 = jnp.min(jnp.where(sel1 & (lg == ev1), lane, big), axis=-1, keepdims=True)
    t = jnp.exp(ev1 - ev0)
    pe0 = p_top / (1.0 + t)
    pe1 = p_top * t / (1.0 + t)

    l0 = i0 - ROUTE_OFF - EXPERTS_PER_GROUP * g_top
    l1 = i1 - ROUTE_OFF - EXPERTS_PER_GROUP * g_top
    la = jnp.minimum(l0, l1)
    lb = jnp.maximum(l0, l1)
    cls = PAIRS_PER_GROUP * g_top + ((la * (7 - la)) >> 1) + (lb - la - 1)
    wa = jnp.where(l0 < l1, pe0, pe1)
    wb = jnp.where(l0 < l1, pe1, pe0)

    tm = lg.shape[0]
    picked = _bf(jnp.where(lane == cls, 1.0, 0.0))
    strict_lower = _bf(jnp.where(_iota((tm, tm), 1) < _iota((tm, tm), 0), 1.0, 0.0))
    prefix = _dot(strict_lower, picked) + cnt_ref[...]
    rank = jnp.sum(jnp.where(lane == cls, prefix, 0.0), axis=-1, keepdims=True)
    cnt_ref[...] += jnp.sum(picked.astype(jnp.float32), axis=0, keepdims=True)

    wrec = jnp.where(lane == 0, wa, jnp.where(lane == 1, wb, 0.0))
    xn_ref[:, D_MODEL:XROW] = wrec
    route_ref[...] = jnp.where(lane == 0, cls.astype(jnp.float32), jnp.where(lane == 1, rank, 0.0))


def _out_proj(h, oa, ob, o_f, o_b, cgate, lw, consts, tm):
    n = h.shape[0]

    def row(i):
        return (i, 0)

    def full(i):
        return (0, 0)

    return pl.pallas_call(
        _out_proj_kernel,
        out_shape=(jax.ShapeDtypeStruct((n, D_MODEL), jnp.float32),
                   jax.ShapeDtypeStruct((n, XROW), jnp.float32),
                   jax.ShapeDtypeStruct((n, ROUTE_LANES), jnp.float32),
                   jax.ShapeDtypeStruct((1, ROUTE_LANES), jnp.float32)),
        grid=(n // tm,),
        in_specs=[
            pl.BlockSpec((tm, D_MODEL), row),
            pl.BlockSpec((tm, 256), row),
            pl.BlockSpec((tm, 512), row),
            pl.BlockSpec((tm, 256), row),
            pl.BlockSpec((tm, 256), row),
            pl.BlockSpec((tm, 256), row),
            pl.BlockSpec((1, 256), full),
            pl.BlockSpec((256, 256), full),
            pl.BlockSpec((D_MODEL, D_MODEL), full),
            pl.BlockSpec((1, D_MODEL), full),
            pl.BlockSpec((2 * ROUTE_LANES, D_MODEL), full),
        ],
        out_specs=(pl.BlockSpec((tm, D_MODEL), row), pl.BlockSpec((tm, XROW), row),
                   pl.BlockSpec((tm, ROUTE_LANES), row), pl.BlockSpec((1, ROUTE_LANES), full)),
        compiler_params=_cparams(("arbitrary",)), name="out_proj",
    )(h, oa, ob, o_f, o_b, cgate, lw["gc"], consts["m64"], lw["w_out"], lw["g_ffn"], lw["wr_t"])


def _row_copy(src, src_row, dst, dst_row, sem):
    return pltpu.make_async_copy(src.at[pl.ds(src_row, 1), :], dst.at[pl.ds(dst_row, 1), :], sem)


def _scatter_kernel(pos_ref, pad_ref, x_ref, xs_ref, pos_smem, pad_smem, zrow_ref, sem_idx, sem_rows):
    ts = x_ref.shape[0]
    cp = pltpu.make_async_copy(pos_ref, pos_smem, sem_idx)
    cp.start()
    cp.wait()

    def issue(r2, carry):
        _row_copy(x_ref, 2 * r2, xs_ref, pos_smem[0, 0, 2 * r2], sem_rows).start(priority=0)
        _row_copy(x_ref, 2 * r2 + 1, xs_ref, pos_smem[0, 0, 2 * r2 + 1], sem_rows).start(priority=1)
        return carry

    lax.fori_loop(0, ts // 2, issue, 0, unroll=8)

    def drain(r, carry):
        _row_copy(x_ref, 0, xs_ref, 0, sem_rows).wait()
        return carry

    lax.fori_loop(0, ts, drain, 0, unroll=8)

    @pl.when(pl.program_id(0) == pl.num_programs(0) - 1)
    def _():
        zrow_ref[...] = jnp.zeros_like(zrow_ref)
        cz = pltpu.make_async_copy(pad_ref, pad_smem, sem_idx)
        cz.start()
        cz.wait()

        def fill(start_wait):
            for c in range(N_CLASSES):
                end = pad_smem[0, 2 * c]
                gap = pad_smem[0, 2 * c + 1]
                for j in range(SUBLANES - 1):
                    @pl.when(j < (gap & (SUBLANES - 1)))
                    def _(j=j, end=end, gap=gap):
                        cp = _row_copy(zrow_ref, 0, xs_ref, end - gap + j, sem_rows)
                        if start_wait:
                            cp.start()
                        else:
                            cp.wait()
                filled = 0
                run = ZERO_ROWS
                while run >= SUBLANES:
                    @pl.when((gap & run) != 0)
                    def _(end=end, filled=filled, run=run):
                        start = pl.multiple_of(end - filled - run, run)
                        cp = pltpu.make_async_copy(zrow_ref.at[pl.ds(0, run), :],
                                                   xs_ref.at[pl.ds(start, run), :], sem_rows)
                        if start_wait:
                            cp.start()
                        else:
                            cp.wait()
                    filled = filled + (gap & run)
                    run //= 2
            tail = pad_smem[0, 2 * N_CLASSES]
            ntail = pad_smem[0, 2 * N_CLASSES + 1]
            for j in range(2 * N_CLASSES):
                @pl.when(j < ntail)
                def _(j=j):
                    start = pl.multiple_of(tail + ZERO_ROWS * j, ZERO_ROWS)
                    cp = pltpu.make_async_copy(zrow_ref, xs_ref.at[pl.ds(start, ZERO_ROWS), :], sem_rows)
                    if start_wait:
                        cp.start()
                    else:
                        cp.wait()

        fill(True)
        fill(False)


def _scatter_rows(xn, pos, padpos, p_rows, ts):
    n = xn.shape[0]
    npad = padpos.shape[1]
    return pl.pallas_call(
        _scatter_kernel,
        out_shape=jax.ShapeDtypeStruct((p_rows, XROW), jnp.float32),
        grid=(n // ts,),
        in_specs=[
            pl.BlockSpec((1, 1, ts), lambda i: (i, 0, 0)),
            pl.BlockSpec((1, npad), lambda i: (0, 0)),
            pl.BlockSpec((ts, XROW), lambda i: (i, 0)),
        ],
        out_specs=pl.BlockSpec(memory_space=pl.ANY),
        scratch_shapes=[
            pltpu.SMEM((1, 1, ts), jnp.int32),
            pltpu.SMEM((1, npad), jnp.int32),
            pltpu.VMEM((ZERO_ROWS, XROW), jnp.float32),
            pltpu.SemaphoreType.DMA(()),
            pltpu.SemaphoreType.DMA(()),
        ],
        compiler_params=_cparams(("arbitrary",)), name="moe_scatter",
    )(pos, padpos, xn)


def _expert_kernel(ea_ref, eb_ref, nu_ref, x_ref, w1a_ref, w3a_ref, w2a_ref, w1b_ref, w3b_ref, w2b_ref, y_ref):
    @pl.when(pl.program_id(0) < nu_ref[0])
    def _():
        x = _bf(x_ref[:, 0:D_MODEL])
        wrec = x_ref[:, D_MODEL:XROW]
        lane = _iota(wrec.shape, 1)
        wa = jnp.sum(jnp.where(lane == 0, wrec, 0.0), axis=-1, keepdims=True)
        wb = jnp.sum(jnp.where(lane == 1, wrec, 0.0), axis=-1, keepdims=True)

        def expert(w1_ref, w3_ref, w2_ref):
            a = _dot(x, w1_ref[...])
            hid = a * (1.0 / (1.0 + jnp.exp(-a))) * _dot(x, w3_ref[...])
            return _dot(_bf(hid), w2_ref[...])

        y_ref[...] = wa * expert(w1a_ref, w3a_ref, w2a_ref) + wb * expert(w1b_ref, w3b_ref, w2b_ref)

    @pl.when(pl.program_id(0) >= nu_ref[0])
    def _():
        y_ref[...] = jnp.zeros_like(y_ref)


def _experts(xs, tile_ea, tile_eb, n_used, lw, te):
    p_rows = xs.shape[0]

    def row(t, ea, eb, nu):
        return (t, 0)

    def wa(t, ea, eb, nu):
        return (ea[t], 0, 0)

    def wb(t, ea, eb, nu):
        return (eb[t], 0, 0)

    grid_spec = pltpu.PrefetchScalarGridSpec(
        num_scalar_prefetch=3,
        grid=(p_rows // te,),
        in_specs=[
            pl.BlockSpec((te, XROW), row),
            pl.BlockSpec((None, D_MODEL, D_EXPERT), wa),
            pl.BlockSpec((None, D_MODEL, D_EXPERT), wa),
            pl.BlockSpec((None, D_EXPERT, D_MODEL), wa),
            pl.BlockSpec((None, D_MODEL, D_EXPERT), wb),
            pl.BlockSpec((None, D_MODEL, D_EXPERT), wb),
            pl.BlockSpec((None, D_EXPERT, D_MODEL), wb),
        ],
        out_specs=pl.BlockSpec((te, D_MODEL), row),
    )
    return pl.pallas_call(
        _expert_kernel, grid_spec=grid_spec,
        out_shape=jax.ShapeDtypeStruct((p_rows, D_MODEL), jnp.float32),
        compiler_params=_cparams(("arbitrary",)), name="moe_experts",
    )(tile_ea, tile_eb, n_used, xs, lw["w1"], lw["w3"], lw["w2"], lw["w1"], lw["w3"], lw["w2"])


def _combine_ple_kernel(pos_ref, posn_ref, ys_ref, h1_ref, p_ref, gple_ref, wpg_ref, wpp_ref,
                        o_ref, pos_smem, buf_ref, sem_idx, sem_rows):
    i = pl.program_id(0)
    nsteps = pl.num_programs(0)
    tc = h1_ref.shape[0]
    slot = lax.rem(i, 2)

    def issue_tile(src_pos_ref, s):
        cp = pltpu.make_async_copy(src_pos_ref, pos_smem, sem_idx)
        cp.start()
        cp.wait()

        def issue(r2, carry):
            _row_copy(ys_ref, pos_smem[0, 0, 2 * r2], buf_ref.at[s], 2 * r2, sem_rows.at[s]).start(priority=0)
            _row_copy(ys_ref, pos_smem[0, 0, 2 * r2 + 1], buf_ref.at[s], 2 * r2 + 1,
                      sem_rows.at[s]).start(priority=1)
            return carry

        lax.fori_loop(0, tc // 2, issue, 0, unroll=8)

    @pl.when(i == 0)
    def _():
        issue_tile(pos_ref, 0)

    @pl.when(i + 1 < nsteps)
    def _():
        issue_tile(posn_ref, 1 - slot)

    def drain(r, carry):
        _row_copy(ys_ref, 0, buf_ref.at[slot], 0, sem_rows.at[slot]).wait()
        return carry

    lax.fori_loop(0, tc, drain, 0, unroll=8)

    h2 = h1_ref[...] + buf_ref[slot]
    xn = h2 * lax.rsqrt(jnp.mean(h2 * h2, axis=-1, keepdims=True) + EPS) * gple_ref[...]
    gate = 1.0 / (1.0 + jnp.exp(-_dot(_bf(xn), wpg_ref[...])))
    o_ref[...] = h2 + _dot(_bf(p_ref[...]), wpp_ref[...]) * gate


def _combine_ple(ys, pos, h1, p_l, lw, tc):
    n = h1.shape[0]
    nsteps = n // tc

    def row(i):
        return (i, 0)

    def full(i):
        return (0, 0)

    return pl.pallas_call(
        _combine_ple_kernel,
        out_shape=jax.ShapeDtypeStruct((n, D_MODEL), jnp.float32),
        grid=(nsteps,),
        in_specs=[
            pl.BlockSpec((1, 1, tc), lambda i: (i, 0, 0)),
            pl.BlockSpec((1, 1, tc), lambda i: (jnp.minimum(i + 1, nsteps - 1), 0, 0)),
            pl.BlockSpec(memory_space=pl.ANY),
            pl.BlockSpec((tc, D_MODEL), row),
            pl.BlockSpec((tc, PLE_DIM), row),
            pl.BlockSpec((1, D_MODEL), full),
            pl.BlockSpec((D_MODEL, D_MODEL), full),
            pl.BlockSpec((PLE_DIM, D_MODEL), full),
        ],
        out_specs=pl.BlockSpec((tc, D_MODEL), row),
        scratch_shapes=[
            pltpu.SMEM((1, 1, tc), jnp.int32),
            pltpu.VMEM((2, tc, D_MODEL), jnp.float32),
            pltpu.SemaphoreType.DMA(()),
            pltpu.SemaphoreType.DMA((2,)),
        ],
        compiler_params=_cparams(("arbitrary",)), name="moe_combine_ple",
    )(pos, pos, ys, h1, p_l, lw["g_ple"], lw["w_pg"], lw["w_pp"])


def _moe_ple(h1, xn, route, counts, p_l, lw, ts, te):
    n = h1.shape[0]
    p_rows = n + N_CLASSES * te
    i32 = jnp.int32
    cnt = counts[0, 0:N_CLASSES].astype(i32)
    padded = ((cnt + te - 1) // te) * te
    pend = jnp.cumsum(padded)
    pstart = pend - padded
    cls = route[:, 0].astype(i32)
    rank = route[:, 1].astype(i32)
    classes = jnp.arange(N_CLASSES, dtype=i32)
    pos = jnp.sum(jnp.where(cls[:, None] == classes[None, :], pstart[None, :], 0), axis=1) + rank
    pos = pos.reshape(n // ts, 1, ts)
    n_tiles = p_rows // te
    tile_row = jnp.arange(n_tiles, dtype=i32) * te
    tile_cls = jnp.minimum(jnp.sum((pend[None, :] <= tile_row[:, None]).astype(i32), axis=1), N_CLASSES - 1)
    pair = tile_cls % PAIRS_PER_GROUP
    group = tile_cls // PAIRS_PER_GROUP
    pair_hot = pair[:, None] == jnp.arange(PAIRS_PER_GROUP, dtype=i32)[None, :]
    tile_ea = group * EXPERTS_PER_GROUP + jnp.sum(jnp.where(pair_hot, jnp.asarray(PAIR_A, i32)[None, :], 0), axis=1)
    tile_eb = group * EXPERTS_PER_GROUP + jnp.sum(jnp.where(pair_hot, jnp.asarray(PAIR_B, i32)[None, :], 0), axis=1)
    n_used = (pend[-1] // te).astype(i32).reshape(1)
    assert te == 2 * ZERO_ROWS
    padpos = jnp.stack([pend, padded - cnt], axis=1).reshape(2 * N_CLASSES)
    tail = jnp.stack([pend[-1], (p_rows - pend[-1]) // ZERO_ROWS])
    padpos = jnp.pad(jnp.concatenate([padpos, tail]), (0, 128 - 2 * N_CLASSES - 2)).astype(i32)[None, :]

    xs = _scatter_rows(xn, pos, padpos, p_rows, ts)
    ys = _experts(xs, tile_ea.astype(i32), tile_eb.astype(i32), n_used, lw, te)
    return _combine_ple(ys, pos, h1, p_l, lw, ts)


_B_HEAD_ORDER = np.array([0, 4, 1, 5, 2, 6, 3, 7])


def _block_mean_matrix(width, block):
    idx = np.arange(width)
    return ((idx[:, None] // block) == (idx[None, :] // block)).astype(np.float32) / block


def _rotate_half_matrix(width, block):
    half = block // 2
    m = np.zeros((width, width), np.float32)
    for j in range(width):
        if j % block < half:
            m[j + half, j] = -1.0
        else:
            m[j - half, j] = 1.0
    return m


def _constants():
    bf16 = jnp.bfloat16
    return {
        "m32": jnp.asarray(_block_mean_matrix(256, A_DH), bf16),
        "m64": jnp.asarray(_block_mean_matrix(256, HEAD_DIM), bf16),
        "rot": jnp.asarray(_rotate_half_matrix(256, A_DH), bf16),
    }


def _rope_tables(seq):
    def table(pos, dim):
        inv = 1.0 / (ROPE_THETA ** (jnp.arange(0, dim, 2, dtype=jnp.float32) / dim))
        ang = pos.astype(jnp.float32)[:, None] * inv[None, :]
        return jnp.cos(ang), jnp.sin(ang)

    rows = seq // GRID_W
    pos = jnp.arange(seq)
    row = jnp.repeat(jnp.arange(rows), GRID_W)
    col = jnp.tile(jnp.arange(GRID_W), rows)
    c1, s1 = table(pos, A_DH)
    cr, sr = table(row, HEAD_DIM // 2)
    cc, sc = table(col, HEAD_DIM // 2)
    return {
        "cosa": jnp.tile(jnp.concatenate([c1, c1], axis=1), (1, 4)),
        "sina": jnp.tile(jnp.concatenate([s1, s1], axis=1), (1, 4)),
        "cosb": jnp.tile(jnp.concatenate([cr, cr, cc, cc], axis=1), (1, 2)),
        "sinb": jnp.tile(jnp.concatenate([sr, sr, sc, sc], axis=1), (1, 2)),
    }


def _prep_layer(l, g_mix, w_in, a_q_norm, a_k_norm, a_lambda, a_subln, b_q_norm, b_k_norm, c_gate_w,
                c_gate_b, c_out_norm, w_out, g_ffn, w_rg, w_re, w_e1, w_e3, w_e2, g_ple, w_pg, w_pp):
    f32 = jnp.float32
    bf16 = jnp.bfloat16
    aq, ak, av = w_in[:, 0:256], w_in[:, 256:512], w_in[:, 512:768]
    bq, bk, bv = w_in[:, 768:1280], w_in[:, 1280:1408], w_in[:, 1408:1536]
    crest = w_in[:, 1536:2336]
    bq = bq.reshape(D_MODEL, B_HEADS, HEAD_DIM)[:, _B_HEAD_ORDER, :].reshape(D_MODEL, 512)
    w_main = jnp.concatenate(
        [aq, ak, bq, bk, crest, jnp.zeros((D_MODEL, MAIN_COLS - QK_COLS - 800), f32)], axis=1)
    w_vt = jnp.concatenate([av, bv], axis=1).T

    gn = jnp.concatenate([
        jnp.tile(a_q_norm, 8) * (A_DH ** -0.5 * LOG2E),
        jnp.tile(a_k_norm, 8),
        jnp.tile(b_q_norm, 8) * (HEAD_DIM ** -0.5 * LOG2E),
        jnp.tile(b_k_norm, 2),
    ])[None, :]

    gw = jnp.zeros((128, 256), f32)
    gw = gw.at[0:C_GATE_RANK, 0:128].set(c_gate_w[0])
    gw = gw.at[C_GATE_RANK:2 * C_GATE_RANK, 128:256].set(c_gate_w[1])
    gb = jnp.concatenate([c_gate_b[0], c_gate_b[1]])[None, :]

    gq_a = jnp.max(jnp.abs(a_q_norm)) * (A_DH ** -0.5 * LOG2E)
    gq_b = jnp.max(jnp.abs(b_q_norm)) * (HEAD_DIM ** -0.5 * LOG2E)
    bound = 1.05 * jnp.maximum(A_DH * gq_a * jnp.max(jnp.abs(a_k_norm)),
                               HEAD_DIM * gq_b * jnp.max(jnp.abs(b_k_norm)))
    bounded = bound <= SCORE_BOUND

    lam_init = 0.8 - 0.6 * math.exp(-0.3 * l)
    lf = a_lambda.astype(f32)
    lam = jnp.exp(jnp.sum(lf[0] * lf[1])) - jnp.exp(jnp.sum(lf[2] * lf[3])) + lam_init

    wo_b = w_out[256:768].reshape(B_HEADS, HEAD_DIM, D_MODEL)[_B_HEAD_ORDER].reshape(512, D_MODEL)
    w_out_p = jnp.concatenate([w_out[0:256], wo_b, w_out[768:1024]], axis=0)

    wr = jnp.zeros((D_MODEL, ROUTE_LANES), f32)
    wr = wr.at[:, 0:N_GROUPS].set(w_rg)
    wr = wr.at[:, ROUTE_OFF:ROUTE_OFF + N_EXPERTS].set(
        jnp.transpose(w_re, (1, 0, 2)).reshape(D_MODEL, N_EXPERTS))
    wr_hi = wr.astype(bf16)
    wr_lo = (wr - wr_hi.astype(f32)).astype(bf16)

    return {
        "g_mix": g_mix[None, :], "w_main": w_main.astype(bf16), "w_vt": w_vt.astype(bf16), "gn": gn,
        "gw": gw.astype(bf16), "gb": gb, "lam": jnp.reshape(lam, (1, 1)), "bounded": bounded,
        "gsub": (a_subln * (1.0 - lam_init))[:, None],
        "gc": jnp.tile(c_out_norm, C_HEADS)[None, :], "w_out": w_out_p.astype(bf16),
        "g_ffn": g_ffn[None, :], "wr_t": jnp.concatenate([wr_hi.T, wr_lo.T], axis=0),
        "w1": w_e1.astype(bf16), "w3": w_e3.astype(bf16), "w2": w_e2.astype(bf16),
        "g_ple": g_ple[None, :], "w_pg": w_pg.astype(bf16), "w_pp": w_pp.astype(bf16),
    }


def _tile(limit, total):
    t = min(limit, total)
    assert total % t == 0
    return t


def _trunk(x, p, layers, consts):
    batch, seq, _ = x.shape
    n = batch * seq
    tabs = _rope_tables(seq)
    tm = _tile(512, seq)
    ts = _tile(512, seq)
    te = 512
    h = x.reshape(n, D_MODEL)
    for l in range(DEPTH):
        lw = layers[l]
        qa, ka, qb, kb, vt, cqk, la, cv, cgate = _in_proj(h, lw, tabs, consts, seq, tm)
        oa = lax.cond(lw["bounded"],
                      lambda *a: _diff_attn(*a, batch, seq, True),
                      lambda *a: _diff_attn(*a, batch, seq, False),
                      qa, ka, vt, lw["lam"], lw["gsub"])
        ob = lax.cond(lw["bounded"],
                      lambda *a: _gqa_attn(*a, batch, seq, True),
                      lambda *a: _gqa_attn(*a, batch, seq, False),
                      qb, kb, vt)
        gla_ok = GLA_CHUNK * jnp.max(-la) <= GLA_DECAY_BOUND
        o_f, o_b = lax.cond(
            gla_ok,
            lambda *a: (_gla(*a, batch, seq, False, True), _gla(*a, batch, seq, True, True)),
            lambda *a: (_gla(*a, batch, seq, False, False), _gla(*a, batch, seq, True, False)),
            cqk, la, cv)
        h1, xn, route, counts = _out_proj(h, oa, ob, o_f, o_b, cgate, lw, consts, tm)
        h = _moe_ple(h1, xn, route, counts, p[l].reshape(n, PLE_DIM), lw, ts, te)
    return h.reshape(batch, seq, D_MODEL)


def kernel(x_prompt, x_sample, p_prompt, p_sample, g_mix, w_in, a_q_norm, a_k_norm, a_lambda, a_subln, b_q_norm, b_k_norm, c_gate_w, c_gate_b, c_out_norm, w_out, g_ffn, w_router_group, w_router_expert, w_exp_gate, w_exp_up, w_exp_down, g_ple, w_ple_gate, w_ple_proj):
    consts = _constants()
    layers = [
        _prep_layer(l, g_mix[l], w_in[l], a_q_norm[l], a_k_norm[l], a_lambda[l], a_subln[l], b_q_norm[l],
                    b_k_norm[l], c_gate_w[l], c_gate_b[l], c_out_norm[l], w_out[l], g_ffn[l],
                    w_router_group[l], w_router_expert[l], w_exp_gate[l], w_exp_up[l], w_exp_down[l],
                    g_ple[l], w_ple_gate[l], w_ple_proj[l])
        for l in range(DEPTH)
    ]
    y_prompt = _trunk(x_prompt, p_prompt, layers, consts)
    y_sample = _trunk(x_sample, p_sample, layers, consts)
    return (y_prompt, y_sample)
```

```python
import functools
import math

import jax
import jax.numpy as jnp
import numpy as np
from jax import lax
from jax.experimental import pallas as pl
from jax.experimental.pallas import tpu as pltpu

D_MODEL = 1024
DEPTH = 4
HEAD_DIM = 64
A_HEADS = 4
A_DH = 32
B_HEADS = 8
B_KV_HEADS = 2
C_HEADS = 4
C_DK = 32
C_DV = 64
C_GATE_RANK = 16
GATE_TAU = 16.0
GLA_CHUNK = 64
GLA_SUB = 16
GRID_W = 64
ROPE_THETA = 10000.0
PLE_DIM = 256
N_GROUPS = 4
EXPERTS_PER_GROUP = 4
N_EXPERTS = 16
D_EXPERT = 512
EPS = 1e-6
LOG2E = 1.4426950408889634

VMEM_LIMIT_BYTES = 52 * 1024 * 1024
NEG_INF = float("-inf")

QK_COLS = 1152
MAIN_COLS = 2048
VT_HEAD_ROWS = 80
V_ROWS = 384
VT_ROWS = 6 * VT_HEAD_ROWS
TQ_BOUNDED = 512
TQ_SHIFTED = 256
SCORE_BOUND = 60.0
GLA_DECAY_BOUND = 80.0
TB_BOUNDED = 1024
TB_EXACT = 256
ZERO_ROWS = 256
SUBLANES = 8
ROUTE_LANES = 128
ROUTE_OFF = 4
PAIRS_PER_GROUP = 6
N_CLASSES = N_GROUPS * PAIRS_PER_GROUP
PAIR_A = (0, 0, 0, 1, 1, 2)
PAIR_B = (1, 2, 3, 2, 3, 3)
XROW = D_MODEL + 128


def _cparams(sem):
    return pltpu.CompilerParams(dimension_semantics=sem, vmem_limit_bytes=VMEM_LIMIT_BYTES)


def _bf(x):
    return x.astype(jnp.bfloat16)


def _dot(a, b):
    return jnp.dot(a, b, preferred_element_type=jnp.float32)


def _dot_nt(a, b):
    return lax.dot_general(a, b, (((1,), (1,)), ((), ())), preferred_element_type=jnp.float32)


def _dot_tn(a, b):
    return lax.dot_general(a, b, (((0,), (0,)), ((), ())), preferred_element_type=jnp.float32)


def _iota(shape, dim):
    return lax.broadcasted_iota(jnp.int32, shape, dim)


def _in_proj_kernel(h_ref, gmix_ref, w_ref, wvt_ref, gn_ref, cosa_ref, sina_ref, cosb_ref, sinb_ref,
                    m32_ref, m64_ref, rot_ref, gw_ref, gb_ref,
                    qa_ref, ka_ref, qb_ref, kb_ref, vt_ref, cqk_ref, la_ref, cv_ref, cg_ref):
    x = h_ref[...]
    a = x * lax.rsqrt(jnp.mean(x * x, axis=-1, keepdims=True) + EPS) * gmix_ref[...]
    ab = _bf(a)
    proj = _dot(ab, w_ref[...])
    vt = _bf(_dot_nt(wvt_ref[...], ab))
    ones = jnp.ones((VT_HEAD_ROWS - HEAD_DIM, vt.shape[1]), jnp.bfloat16)
    for hh in range(VT_ROWS // VT_HEAD_ROWS):
        vt_ref[VT_HEAD_ROWS * hh:VT_HEAD_ROWS * hh + HEAD_DIM, :] = vt[HEAD_DIM * hh:HEAD_DIM * (hh + 1), :]
        vt_ref[VT_HEAD_ROWS * hh + HEAD_DIM:VT_HEAD_ROWS * (hh + 1), :] = ones

    cosa = cosa_ref[...]
    sina = sina_ref[...]
    cosb = cosb_ref[...]
    sinb = sinb_ref[...]
    cosa2 = jnp.concatenate([cosa, cosa], axis=1)
    sina2 = jnp.concatenate([sina, sina], axis=1)
    cosb2 = jnp.concatenate([cosb, cosb], axis=1)
    sinb2 = jnp.concatenate([sinb, sinb], axis=1)

    def qk_slab(off, width, mean_mat, cos, sin, out_ref, out_off):
        xs = proj[:, off:off + width]
        ms = _dot(_bf(xs * xs), mean_mat)
        xn = xs * lax.rsqrt(ms + EPS) * gn_ref[:, off:off + width]
        rot = _dot(_bf(xn), rot_ref[0:width, 0:width])
        out_ref[:, out_off:out_off + width] = _bf(xn * cos + rot * sin)

    m32 = m32_ref[...]
    m64 = m64_ref[...]
    qk_slab(0, 256, m32, cosa2, sina2, qa_ref, 0)
    qk_slab(256, 256, m32, cosa2, sina2, ka_ref, 0)
    qk_slab(512, 256, m64, cosb2, sinb2, qb_ref, 0)
    qk_slab(768, 256, m64, cosb2, sinb2, qb_ref, 256)
    qk_slab(1024, 128, m64_ref[0:128, 0:128], cosb, sinb, kb_ref, 0)

    c0 = QK_COLS
    cqk_ref[:, 0:128] = proj[:, c0:c0 + 128] * (C_DK ** -0.5)
    cqk_ref[:, 128:256] = proj[:, c0 + 128:c0 + 256]
    cv_ref[...] = proj[:, c0 + 256:c0 + 512]
    cg = proj[:, c0 + 512:c0 + 768]
    cg_ref[...] = cg * (1.0 / (1.0 + jnp.exp(-cg)))
    clr = proj[:, c0 + 768:c0 + 896]
    z = _dot(_bf(clr), gw_ref[...]) + gb_ref[...]
    la_ref[...] = (jnp.minimum(z, 0.0) - jnp.log(1.0 + jnp.exp(-jnp.abs(z)))) * (1.0 / GATE_TAU)


def _in_proj(h, lw, tabs, consts, seq, tm):
    n = h.shape[0]
    nt = n // tm
    spb = seq // tm

    def row(i):
        return (i, 0)

    def full(i):
        return (0, 0)

    def tab(i):
        return (i % spb, 0)

    f32 = jnp.float32
    bf16 = jnp.bfloat16
    out_shape = (
        jax.ShapeDtypeStruct((n, 256), bf16),
        jax.ShapeDtypeStruct((n, 256), bf16),
        jax.ShapeDtypeStruct((n, 512), bf16),
        jax.ShapeDtypeStruct((n, 128), bf16),
        jax.ShapeDtypeStruct((VT_ROWS, n), bf16),
        jax.ShapeDtypeStruct((n, 256), f32),
        jax.ShapeDtypeStruct((n, 256), f32),
        jax.ShapeDtypeStruct((n, 256), f32),
        jax.ShapeDtypeStruct((n, 256), f32),
    )
    in_specs = [
        pl.BlockSpec((tm, D_MODEL), row),
        pl.BlockSpec((1, D_MODEL), full),
        pl.BlockSpec((D_MODEL, MAIN_COLS), full),
        pl.BlockSpec((V_ROWS, D_MODEL), full),
        pl.BlockSpec((1, QK_COLS), full),
        pl.BlockSpec((tm, 128), tab),
        pl.BlockSpec((tm, 128), tab),
        pl.BlockSpec((tm, 128), tab),
        pl.BlockSpec((tm, 128), tab),
        pl.BlockSpec((256, 256), full),
        pl.BlockSpec((256, 256), full),
        pl.BlockSpec((256, 256), full),
        pl.BlockSpec((128, 256), full),
        pl.BlockSpec((1, 256), full),
    ]
    out_specs = (
        pl.BlockSpec((tm, 256), row),
        pl.BlockSpec((tm, 256), row),
        pl.BlockSpec((tm, 512), row),
        pl.BlockSpec((tm, 128), row),
        pl.BlockSpec((VT_ROWS, tm), lambda i: (0, i)),
        pl.BlockSpec((tm, 256), row),
        pl.BlockSpec((tm, 256), row),
        pl.BlockSpec((tm, 256), row),
        pl.BlockSpec((tm, 256), row),
    )
    return pl.pallas_call(
        _in_proj_kernel, out_shape=out_shape, grid=(nt,), in_specs=in_specs, out_specs=out_specs,
        compiler_params=_cparams(("parallel",)), name="in_proj",
    )(h, lw["g_mix"], lw["w_main"], lw["w_vt"], lw["gn"], tabs["cosa"], tabs["sina"], tabs["cosb"],
      tabs["sinb"], consts["m32"], consts["m64"], consts["rot"], lw["gw"], lw["gb"])


def _softmax_pv(k, qm, vt, bounded):
    st = _dot_nt(k, qm)
    if bounded:
        p = jnp.exp2(st)
    else:
        p = jnp.exp2(st - jnp.max(st, axis=0, keepdims=True))
    ot = _dot(vt, _bf(p))
    return ot[0:HEAD_DIM, :] / ot[HEAD_DIM:HEAD_DIM + 1, :]


def _diff_attn_kernel(q_ref, k_ref, vt_ref, lam_ref, gsub_ref, o_ref, *, bounded):
    q = q_ref[...]
    k = k_ref[...]
    lane = _iota(q.shape, 1)
    zero = jnp.zeros_like(q)
    lam = lam_ref[...]
    gsub = gsub_ref[...]
    outs = []
    for h in range(A_HEADS):
        vt = vt_ref[VT_HEAD_ROWS * h:VT_HEAD_ROWS * (h + 1), :]
        comp = []
        for c in range(2):
            lo = A_DH * (2 * h + c)
            qm = jnp.where((lane >= lo) & (lane < lo + A_DH), q, zero)
            comp.append(_softmax_pv(k, qm, vt, bounded))
        od = comp[0] - lam * comp[1]
        r = lax.rsqrt(jnp.mean(od * od, axis=0, keepdims=True) + EPS)
        outs.append(od * r * gsub)
    ot = jnp.concatenate(outs, axis=0)
    o_ref[...] = _bf(ot.T)


def _gqa_attn_kernel(q_ref, k_ref, vt_ref, o_ref, *, bounded):
    k = k_ref[...]
    outs = []
    for j in range(B_HEADS // B_KV_HEADS):
        q = q_ref[:, 128 * j:128 * (j + 1)]
        lane = _iota(q.shape, 1)
        zero = jnp.zeros_like(q)
        for g in range(B_KV_HEADS):
            qm = jnp.where((lane >= HEAD_DIM * g) & (lane < HEAD_DIM * (g + 1)), q, zero)
            vt = vt_ref[VT_HEAD_ROWS * g:VT_HEAD_ROWS * (g + 1), :]
            outs.append(_softmax_pv(k, qm, vt, bounded))
    ot = jnp.concatenate(outs, axis=0)
    o_ref[...] = _bf(ot.T)


def _diff_attn(qa, ka, vt, lam, gsub, batch, seq, bounded):
    n = qa.shape[0]
    tq = _tile(TQ_BOUNDED if bounded else TQ_SHIFTED, seq)
    nq = seq // tq
    return pl.pallas_call(
        functools.partial(_diff_attn_kernel, bounded=bounded),
        out_shape=jax.ShapeDtypeStruct((n, 256), jnp.bfloat16),
        grid=(batch, nq),
        in_specs=[
            pl.BlockSpec((tq, 256), lambda b, i: (b * nq + i, 0)),
            pl.BlockSpec((seq, 256), lambda b, i: (b, 0)),
            pl.BlockSpec((A_HEADS * VT_HEAD_ROWS, seq), lambda b, i: (0, b)),
            pl.BlockSpec((1, 1), lambda b, i: (0, 0)),
            pl.BlockSpec((HEAD_DIM, 1), lambda b, i: (0, 0)),
        ],
        out_specs=pl.BlockSpec((tq, 256), lambda b, i: (b * nq + i, 0)),
        compiler_params=_cparams(("parallel", "parallel")),
        name="diff_attn" if bounded else "diff_attn_shifted",
    )(qa, ka, vt, lam, gsub)


def _gqa_attn(qb, kb, vt, batch, seq, bounded):
    n = qb.shape[0]
    tq = _tile(TQ_BOUNDED if bounded else TQ_SHIFTED, seq)
    nq = seq // tq
    return pl.pallas_call(
        functools.partial(_gqa_attn_kernel, bounded=bounded),
        out_shape=jax.ShapeDtypeStruct((n, 512), jnp.bfloat16),
        grid=(batch, nq),
        in_specs=[
            pl.BlockSpec((tq, 512), lambda b, i: (b * nq + i, 0)),
            pl.BlockSpec((seq, 128), lambda b, i: (b, 0)),
            pl.BlockSpec((B_KV_HEADS * VT_HEAD_ROWS, seq), lambda b, i: (A_HEADS // B_KV_HEADS, b)),
        ],
        out_specs=pl.BlockSpec((tq, 512), lambda b, i: (b * nq + i, 0)),
        compiler_params=_cparams(("parallel", "parallel")),
        name="gqa_attn" if bounded else "gqa_attn_shifted",
    )(qb, kb, vt)


def _split3(x):
    hi = _bf(x)
    r1 = x - hi.astype(jnp.float32)
    mid = _bf(r1)
    lo = _bf(r1 - mid.astype(jnp.float32))
    return hi, mid, lo


def _gla_kernel(qk_ref, g_ref, v_ref, o_ref, st_ref, *, nchunk, rev):
    C = GLA_CHUNK
    SB = GLA_SUB
    nsub = C // SB

    @pl.when(pl.program_id(1) == 0)
    def _():
        st_ref[...] = jnp.zeros_like(st_ref)

    r64 = _iota((C, C), 0)
    c64 = _iota((C, C), 1)
    tri = _bf(jnp.where((c64 >= r64) if rev else (c64 <= r64), 1.0, 0.0))
    hm_k = (_iota((4 * C, 128), 0) >> 6) == (_iota((4 * C, 128), 1) >> 5)
    hm_v = (_iota((4 * C, 256), 0) >> 6) == (_iota((4 * C, 256), 1) >> 6)
    ones_exp = _bf(jnp.where((_iota((128, 256), 0) >> 5) == (_iota((128, 256), 1) >> 6), 1.0, 0.0))
    rowc = _iota((C, 128), 0)
    rows = _iota((SB, 128), 0)

    order = range(nchunk - 1, -1, -1) if rev else range(nchunk)
    for ci in order:
        r0 = ci * C
        q = qk_ref[r0:r0 + C, 0:128]
        k = qk_ref[r0:r0 + C, 128:256]
        g = g_ref[r0:r0 + C, :]
        v = v_ref[r0:r0 + C, :]
        g1, g2, g3 = _split3(g)
        b = _dot(tri, g1) + _dot(tri, g2) + _dot(tri, g3)

        st = st_ref[...]
        o = _dot_nt(_bf(q * jnp.exp(b)), _bf(st))
        bl = b[0:1, :] if rev else b[C - 1:C, :]
        ke = k * jnp.exp(bl - b)
        dst = _dot_tn(_bf(v), _bf(ke))
        st_ref[...] = st * jnp.exp(bl) + jnp.where(hm_k, dst, 0.0)

        att = jnp.zeros((C, 4 * C), jnp.float32)
        for blk in (range(0, nsub - 1) if rev else range(1, nsub)):
            if rev:
                ref = b[SB * blk + SB - 1:SB * blk + SB, :]
                kmask = rowc >= SB * (blk + 1)
            else:
                ref = b[SB * blk:SB * blk + 1, :]
                kmask = rowc < SB * blk
            qmask = (rowc >= SB * blk) & (rowc < SB * (blk + 1))
            qd = q * jnp.exp(jnp.where(qmask, b - ref, NEG_INF))
            kd = k * jnp.exp(jnp.where(kmask, ref - b, NEG_INF))
            kexp = jnp.where(hm_k, jnp.concatenate([kd] * 4, axis=0), 0.0)
            att = att + _dot_nt(_bf(qd), _bf(kexp))
        vexp = jnp.where(hm_v, jnp.concatenate([v] * 4, axis=0), 0.0)
        o = o + _dot(_bf(att), _bf(vexp))

        diag = []
        for blk in range(nsub):
            s0 = SB * blk
            bs = b[s0:s0 + SB, :]
            qs = q[s0:s0 + SB, :]
            ks = k[s0:s0 + SB, :]
            vs = v[s0:s0 + SB, :]
            ws = []
            for j in range(SB):
                keep = (rows <= j) if rev else (rows >= j)
                e = jnp.exp(jnp.where(keep, bs - bs[j:j + 1, :], NEG_INF))
                ws.append(e * qs * ks[j:j + 1, :])
            p = _dot(_bf(jnp.concatenate(ws, axis=0)), ones_exp)
            od = p[0:SB, :] * vs[0:1, :]
            for j in range(1, SB):
                od = od + p[SB * j:SB * (j + 1), :] * vs[j:j + 1, :]
            diag.append(od)
        o_ref[r0:r0 + C, :] = o + jnp.concatenate(diag, axis=0)


def _gla_bounded_kernel(qk_ref, g_ref, v_ref, o_ref, st_ref, *, nchunk, rev):
    C = GLA_CHUNK

    @pl.when(pl.program_id(1) == 0)
    def _():
        st_ref[...] = jnp.zeros_like(st_ref)

    r64 = _iota((C, C), 0)
    c64 = _iota((C, C), 1)
    tri = _bf(jnp.where((c64 >= r64) if rev else (c64 <= r64), 1.0, 0.0))
    hm_k = (_iota((4 * C, 128), 0) >> 6) == (_iota((4 * C, 128), 1) >> 5)
    hm_v = (_iota((4 * C, 256), 0) >> 6) == (_iota((4 * C, 256), 1) >> 6)
    qpos = _iota((C, 4 * C), 0)
    kpos = _iota((C, 4 * C), 1) & (C - 1)
    keep = (kpos >= qpos) if rev else (kpos <= qpos)

    order = range(nchunk - 1, -1, -1) if rev else range(nchunk)
    rng = [(ci * C, (ci + 1) * C) for ci in order]
    bs = []
    for lo, hi in rng:
        g1, g2, g3 = _split3(g_ref[lo:hi, :])
        bs.append(_dot(tri, g1) + _dot(tri, g2) + _dot(tri, g3))
    qes = [_bf(qk_ref[lo:hi, 0:128] * jnp.exp(b)) for (lo, hi), b in zip(rng, bs)]
    kds = [qk_ref[lo:hi, 128:256] * jnp.exp(-b) for (lo, hi), b in zip(rng, bs)]
    ebls = [jnp.exp(b[0:1, :] if rev else b[C - 1:C, :]) for b in bs]
    vbs = [_bf(v_ref[lo:hi, :]) for lo, hi in rng]
    atts = [jnp.where(keep, _dot_nt(qe, _bf(jnp.where(hm_k, jnp.concatenate([kd] * 4, axis=0), 0.0))), 0.0)
            for qe, kd in zip(qes, kds)]
    zero = jnp.zeros((), jnp.bfloat16)
    intras = [_dot(_bf(att), jnp.where(hm_v, jnp.concatenate([vb] * 4, axis=0), zero))
              for att, vb in zip(atts, vbs)]
    dsts = [jnp.where(hm_k, _dot_tn(vb, _bf(kd * ebl)), 0.0) for vb, kd, ebl in zip(vbs, kds, ebls)]

    st = st_ref[...]
    for (lo, hi), qe, o_intra, ebl, dst in zip(rng, qes, intras, ebls, dsts):
        o_ref[lo:hi, :] = o_intra + _dot_nt(qe, _bf(st))
        st = st * ebl + dst
    st_ref[...] = st


def _gla(cqk, la, cv, batch, seq, rev, bounded):
    n = cqk.shape[0]
    tb = _tile(TB_BOUNDED if bounded else TB_EXACT, seq)
    nb = seq // tb

    def blk(b, i):
        return (b * nb + (nb - 1 - i if rev else i), 0)

    def gblk(b, i):
        return (b * nb + (nb - 1 - i if rev else i), 1 if rev else 0)

    return pl.pallas_call(
        functools.partial(_gla_bounded_kernel if bounded else _gla_kernel, nchunk=tb // GLA_CHUNK, rev=rev),
        out_shape=jax.ShapeDtypeStruct((n, 256), jnp.float32),
        grid=(batch, nb),
        in_specs=[
            pl.BlockSpec((tb, 256), blk),
            pl.BlockSpec((tb, 128), gblk),
            pl.BlockSpec((tb, 256), blk),
        ],
        out_specs=pl.BlockSpec((tb, 256), blk),
        scratch_shapes=[pltpu.VMEM((256, 128), jnp.float32)],
        compiler_params=_cparams(("parallel", "arbitrary")),
        name=("gla_bwd" if rev else "gla_fwd") + ("" if bounded else "_exact"),
    )(cqk, la, cv)


def _out_proj_kernel(h_ref, oa_ref, ob_ref, of_ref, obw_ref, cg_ref, gc_ref, m64_ref, wo_ref, gffn_ref,
                     wrt_ref, h1_ref, xn_ref, route_ref, cnt_ref):
    @pl.when(pl.program_id(0) == 0)
    def _():
        cnt_ref[...] = jnp.zeros_like(cnt_ref)

    oc = of_ref[...] + obw_ref[...]
    ms = _dot(_bf(oc * oc), m64_ref[...])
    oc = oc * lax.rsqrt(ms + EPS) * gc_ref[...] * cg_ref[...]
    h1 = (h_ref[...] + _dot(oa_ref[...], wo_ref[0:256, :]) + _dot(ob_ref[...], wo_ref[256:768, :])
          + _dot(_bf(oc), wo_ref[768:1024, :]))
    h1_ref[...] = h1
    xn = h1 * lax.rsqrt(jnp.mean(h1 * h1, axis=-1, keepdims=True) + EPS) * gffn_ref[...]
    xn_ref[:, 0:D_MODEL] = xn
    xh = _bf(xn)
    xl = _bf(xn - xh.astype(jnp.float32))
    lgt = _dot_nt(wrt_ref[...], xh)
    lgt = lgt[0:ROUTE_LANES, :] + lgt[ROUTE_LANES:2 * ROUTE_LANES, :] + _dot_nt(wrt_ref[0:ROUTE_LANES, :], xl)
    lg = lgt.T

    lane = _iota(lg.shape, 1)
    big = jnp.int32(1 << 20)
    is_g = lane < N_GROUPS
    mg = jnp.max(jnp.where(is_g, lg, NEG_INF), axis=-1, keepdims=True)
    sg = jnp.sum(jnp.where(is_g, jnp.exp(lg - mg), 0.0), axis=-1, keepdims=True)
    p_top = 1.0 / sg
    g_top = jnp.min(jnp.where(is_g & (lg == mg), lane, big), axis=-1, keepdims=True)
    e_lo = ROUTE_OFF + EXPERTS_PER_GROUP * g_top
    sel = (lane >= e_lo) & (lane < e_lo + EXPERTS_PER_GROUP)
    ev0 = jnp.max(jnp.where(sel, lg, NEG_INF), axis=-1, keepdims=True)
    i0 = jnp.min(jnp.where(sel & (lg == ev0), lane, big), axis=-1, keepdims=True)
    sel1 = sel & (lane != i0)
    ev1 = jnp.max(jnp.where(sel1, lg, NEG_INF), axis=-1, keepdims=True)
    i1 = jnp.min(jnp.where(sel1 & (lg == ev1), lane, big), axis=-1, keepdims=True)
    t = jnp.exp(ev1 - ev0)
    pe0 = p_top / (1.0 + t)
    pe1 = p_top * t / (1.0 + t)

    l0 = i0 - ROUTE_OFF - EXPERTS_PER_GROUP * g_top
    l1 = i1 - ROUTE_OFF - EXPERTS_PER_GROUP * g_top
    la = jnp.minimum(l0, l1)
    lb = jnp.maximum(l0, l1)
    cls = PAIRS_PER_GROUP * g_top + ((la * (7 - la)) >> 1) + (lb - la - 1)
    wa = jnp.where(l0 < l1, pe0, pe1)
    wb = jnp.where(l0 < l1, pe1, pe0)

    tm = lg.shape[0]
    picked = _bf(jnp.where(lane == cls, 1.0, 0.0))
    strict_lower = _bf(jnp.where(_iota((tm, tm), 1) < _iota((tm, tm), 0), 1.0, 0.0))
    prefix = _dot(strict_lower, picked) + cnt_ref[...]
    rank = jnp.sum(jnp.where(lane == cls, prefix, 0.0), axis=-1, keepdims=True)
    cnt_ref[...] += jnp.sum(picked.astype(jnp.float32), axis=0, keepdims=True)

    wrec = jnp.where(lane == 0, wa, jnp.where(lane == 1, wb, 0.0))
    xn_ref[:, D_MODEL:XROW] = wrec
    route_ref[...] = jnp.where(lane == 0, cls.astype(jnp.float32), jnp.where(lane == 1, rank, 0.0))


def _out_proj(h, oa, ob, o_f, o_b, cgate, lw, consts, tm):
    n = h.shape[0]

    def row(i):
        return (i, 0)

    def full(i):
        return (0, 0)

    return pl.pallas_call(
        _out_proj_kernel,
        out_shape=(jax.ShapeDtypeStruct((n, D_MODEL), jnp.float32),
                   jax.ShapeDtypeStruct((n, XROW), jnp.float32),
                   jax.ShapeDtypeStruct((n, ROUTE_LANES), jnp.float32),
                   jax.ShapeDtypeStruct((1, ROUTE_LANES), jnp.float32)),
        grid=(n // tm,),
        in_specs=[
            pl.BlockSpec((tm, D_MODEL), row),
            pl.BlockSpec((tm, 256), row),
            pl.BlockSpec((tm, 512), row),
            pl.BlockSpec((tm, 256), row),
            pl.BlockSpec((tm, 256), row),
            pl.BlockSpec((tm, 256), row),
            pl.BlockSpec((1, 256), full),
            pl.BlockSpec((256, 256), full),
            pl.BlockSpec((D_MODEL, D_MODEL), full),
            pl.BlockSpec((1, D_MODEL), full),
            pl.BlockSpec((2 * ROUTE_LANES, D_MODEL), full),
        ],
        out_specs=(pl.BlockSpec((tm, D_MODEL), row), pl.BlockSpec((tm, XROW), row),
                   pl.BlockSpec((tm, ROUTE_LANES), row), pl.BlockSpec((1, ROUTE_LANES), full)),
        compiler_params=_cparams(("arbitrary",)), name="out_proj",
    )(h, oa, ob, o_f, o_b, cgate, lw["gc"], consts["m64"], lw["w_out"], lw["g_ffn"], lw["wr_t"])


def _row_copy(src, src_row, dst, dst_row, sem):
    return pltpu.make_async_copy(src.at[pl.ds(src_row, 1), :], dst.at[pl.ds(dst_row, 1), :], sem)


def _scatter_kernel(pos_ref, pad_ref, x_ref, xs_ref, pos_smem, pad_smem, zrow_ref, sem_idx, sem_rows):
    ts = x_ref.shape[0]
    cp = pltpu.make_async_copy(pos_ref, pos_smem, sem_idx)
    cp.start()
    cp.wait()

    def issue(r2, carry):
        _row_copy(x_ref, 2 * r2, xs_ref, pos_smem[0, 0, 2 * r2], sem_rows).start(priority=0)
        _row_copy(x_ref, 2 * r2 + 1, xs_ref, pos_smem[0, 0, 2 * r2 + 1], sem_rows).start(priority=1)
        return carry

    lax.fori_loop(0, ts // 2, issue, 0, unroll=8)

    def drain(r, carry):
        _row_copy(x_ref, 0, xs_ref, 0, sem_rows).wait()
        return carry

    lax.fori_loop(0, ts, drain, 0, unroll=8)

    @pl.when(pl.program_id(0) == pl.num_programs(0) - 1)
    def _():
        zrow_ref[...] = jnp.zeros_like(zrow_ref)
        cz = pltpu.make_async_copy(pad_ref, pad_smem, sem_idx)
        cz.start()
        cz.wait()

        def fill(start_wait):
            for c in range(N_CLASSES):
                end = pad_smem[0, 2 * c]
                gap = pad_smem[0, 2 * c + 1]
                for j in range(SUBLANES - 1):
                    @pl.when(j < (gap & (SUBLANES - 1)))
                    def _(j=j, end=end, gap=gap):
                        cp = _row_copy(zrow_ref, 0, xs_ref, end - gap + j, sem_rows)
                        if start_wait:
                            cp.start()
                        else:
                            cp.wait()
                filled = 0
                run = ZERO_ROWS
                while run >= SUBLANES:
                    @pl.when((gap & run) != 0)
                    def _(end=end, filled=filled, run=run):
                        start = pl.multiple_of(end - filled - run, run)
                        cp = pltpu.make_async_copy(zrow_ref.at[pl.ds(0, run), :],
                                                   xs_ref.at[pl.ds(start, run), :], sem_rows)
                        if start_wait:
                            cp.start()
                        else:
                            cp.wait()
                    filled = filled + (gap & run)
                    run //= 2
            tail = pad_smem[0, 2 * N_CLASSES]
            ntail = pad_smem[0, 2 * N_CLASSES + 1]
            for j in range(2 * N_CLASSES):
                @pl.when(j < ntail)
                def _(j=j):
                    start = pl.multiple_of(tail + ZERO_ROWS * j, ZERO_ROWS)
                    cp = pltpu.make_async_copy(zrow_ref, xs_ref.at[pl.ds(start, ZERO_ROWS), :], sem_rows)
                    if start_wait:
                        cp.start()
                    else:
                        cp.wait()

        fill(True)
        fill(False)


def _scatter_rows(xn, pos, padpos, p_rows, ts):
    n = xn.shape[0]
    npad = padpos.shape[1]
    return pl.pallas_call(
        _scatter_kernel,
        out_shape=jax.ShapeDtypeStruct((p_rows, XROW), jnp.float32),
        grid=(n // ts,),
        in_specs=[
            pl.BlockSpec((1, 1, ts), lambda i: (i, 0, 0)),
            pl.BlockSpec((1, npad), lambda i: (0, 0)),
            pl.BlockSpec((ts, XROW), lambda i: (i, 0)),
        ],
        out_specs=pl.BlockSpec(memory_space=pl.ANY),
        scratch_shapes=[
            pltpu.SMEM((1, 1, ts), jnp.int32),
            pltpu.SMEM((1, npad), jnp.int32),
            pltpu.VMEM((ZERO_ROWS, XROW), jnp.float32),
            pltpu.SemaphoreType.DMA(()),
            pltpu.SemaphoreType.DMA(()),
        ],
        compiler_params=_cparams(("arbitrary",)), name="moe_scatter",
    )(pos, padpos, xn)


def _expert_kernel(ea_ref, eb_ref, nu_ref, x_ref, w1a_ref, w3a_ref, w2a_ref, w1b_ref, w3b_ref, w2b_ref, y_ref):
    @pl.when(pl.program_id(0) < nu_ref[0])
    def _():
        x = _bf(x_ref[:, 0:D_MODEL])
        wrec = x_ref[:, D_MODEL:XROW]
        lane = _iota(wrec.shape, 1)
        wa = jnp.sum(jnp.where(lane == 0, wrec, 0.0), axis=-1, keepdims=True)
        wb = jnp.sum(jnp.where(lane == 1, wrec, 0.0), axis=-1, keepdims=True)

        def expert(w1_ref, w3_ref, w2_ref):
            a = _dot(x, w1_ref[...])
            hid = a * (1.0 / (1.0 + jnp.exp(-a))) * _dot(x, w3_ref[...])
            return _dot(_bf(hid), w2_ref[...])

        y_ref[...] = wa * expert(w1a_ref, w3a_ref, w2a_ref) + wb * expert(w1b_ref, w3b_ref, w2b_ref)

    @pl.when(pl.program_id(0) >= nu_ref[0])
    def _():
        y_ref[...] = jnp.zeros_like(y_ref)


def _experts(xs, tile_ea, tile_eb, n_used, lw, te):
    p_rows = xs.shape[0]

    def row(t, ea, eb, nu):
        return (t, 0)

    def wa(t, ea, eb, nu):
        return (ea[t], 0, 0)

    def wb(t, ea, eb, nu):
        return (eb[t], 0, 0)

    grid_spec = pltpu.PrefetchScalarGridSpec(
        num_scalar_prefetch=3,
        grid=(p_rows // te,),
        in_specs=[
            pl.BlockSpec((te, XROW), row),
            pl.BlockSpec((None, D_MODEL, D_EXPERT), wa),
            pl.BlockSpec((None, D_MODEL, D_EXPERT), wa),
            pl.BlockSpec((None, D_EXPERT, D_MODEL), wa),
            pl.BlockSpec((None, D_MODEL, D_EXPERT), wb),
            pl.BlockSpec((None, D_MODEL, D_EXPERT), wb),
            pl.BlockSpec((None, D_EXPERT, D_MODEL), wb),
        ],
        out_specs=pl.BlockSpec((te, D_MODEL), row),
    )
    return pl.pallas_call(
        _expert_kernel, grid_spec=grid_spec,
        out_shape=jax.ShapeDtypeStruct((p_rows, D_MODEL), jnp.float32),
        compiler_params=_cparams(("arbitrary",)), name="moe_experts",
    )(tile_ea, tile_eb, n_used, xs, lw["w1"], lw["w3"], lw["w2"], lw["w1"], lw["w3"], lw["w2"])


def _combine_ple_kernel(pos_ref, posn_ref, ys_ref, h1_ref, p_ref, gple_ref, wpg_ref, wpp_ref,
                        o_ref, pos_smem, buf_ref, sem_idx, sem_rows):
    i = pl.program_id(0)
    nsteps = pl.num_programs(0)
    tc = h1_ref.shape[0]
    slot = lax.rem(i, 2)

    def issue_tile(src_pos_ref, s):
        cp = pltpu.make_async_copy(src_pos_ref, pos_smem, sem_idx)
        cp.start()
        cp.wait()

        def issue(r2, carry):
            _row_copy(ys_ref, pos_smem[0, 0, 2 * r2], buf_ref.at[s], 2 * r2, sem_rows.at[s]).start(priority=0)
            _row_copy(ys_ref, pos_smem[0, 0, 2 * r2 + 1], buf_ref.at[s], 2 * r2 + 1,
                      sem_rows.at[s]).start(priority=1)
            return carry

        lax.fori_loop(0, tc // 2, issue, 0, unroll=8)

    @pl.when(i == 0)
    def _():
        issue_tile(pos_ref, 0)

    @pl.when(i + 1 < nsteps)
    def _():
        issue_tile(posn_ref, 1 - slot)

    def drain(r, carry):
        _row_copy(ys_ref, 0, buf_ref.at[slot], 0, sem_rows.at[slot]).wait()
        return carry

    lax.fori_loop(0, tc, drain, 0, unroll=8)

    h2 = h1_ref[...] + buf_ref[slot]
    xn = h2 * lax.rsqrt(jnp.mean(h2 * h2, axis=-1, keepdims=True) + EPS) * gple_ref[...]
    gate = 1.0 / (1.0 + jnp.exp(-_dot(_bf(xn), wpg_ref[...])))
    o_ref[...] = h2 + _dot(_bf(p_ref[...]), wpp_ref[...]) * gate


def _combine_ple(ys, pos, h1, p_l, lw, tc):
    n = h1.shape[0]
    nsteps = n // tc

    def row(i):
        return (i, 0)

    def full(i):
        return (0, 0)

    return pl.pallas_call(
        _combine_ple_kernel,
        out_shape=jax.ShapeDtypeStruct((n, D_MODEL), jnp.float32),
        grid=(nsteps,),
        in_specs=[
            pl.BlockSpec((1, 1, tc), lambda i: (i, 0, 0)),
            pl.BlockSpec((1, 1, tc), lambda i: (jnp.minimum(i + 1, nsteps - 1), 0, 0)),
            pl.BlockSpec(memory_space=pl.ANY),
            pl.BlockSpec((tc, D_MODEL), row),
            pl.BlockSpec((tc, PLE_DIM), row),
            pl.BlockSpec((1, D_MODEL), full),
            pl.BlockSpec((D_MODEL, D_MODEL), full),
            pl.BlockSpec((PLE_DIM, D_MODEL), full),
        ],
        out_specs=pl.BlockSpec((tc, D_MODEL), row),
        scratch_shapes=[
            pltpu.SMEM((1, 1, tc), jnp.int32),
            pltpu.VMEM((2, tc, D_MODEL), jnp.float32),
            pltpu.SemaphoreType.DMA(()),
            pltpu.SemaphoreType.DMA((2,)),
        ],
        compiler_params=_cparams(("arbitrary",)), name="moe_combine_ple",
    )(pos, pos, ys, h1, p_l, lw["g_ple"], lw["w_pg"], lw["w_pp"])


def _moe_ple(h1, xn, route, counts, p_l, lw, ts, tc, te):
    n = h1.shape[0]
    p_rows = n + N_CLASSES * te
    i32 = jnp.int32
    cnt = counts[0, 0:N_CLASSES].astype(i32)
    padded = ((cnt + te - 1) // te) * te
    pend = jnp.cumsum(padded)
    pstart = pend - padded
    cls = route[:, 0].astype(i32)
    rank = route[:, 1].astype(i32)
    classes = jnp.arange(N_CLASSES, dtype=i32)
    pos = jnp.sum(jnp.where(cls[:, None] == classes[None, :], pstart[None, :], 0), axis=1) + rank
    n_tiles = p_rows // te
    tile_row = jnp.arange(n_tiles, dtype=i32) * te
    tile_cls = jnp.minimum(jnp.sum((pend[None, :] <= tile_row[:, None]).astype(i32), axis=1), N_CLASSES - 1)
    pair = tile_cls % PAIRS_PER_GROUP
    group = tile_cls // PAIRS_PER_GROUP
    pair_hot = pair[:, None] == jnp.arange(PAIRS_PER_GROUP, dtype=i32)[None, :]
    tile_ea = group * EXPERTS_PER_GROUP + jnp.sum(jnp.where(pair_hot, jnp.asarray(PAIR_A, i32)[None, :], 0), axis=1)
    tile_eb = group * EXPERTS_PER_GROUP + jnp.sum(jnp.where(pair_hot, jnp.asarray(PAIR_B, i32)[None, :], 0), axis=1)
    n_used = (pend[-1] // te).astype(i32).reshape(1)
    assert te == 2 * ZERO_ROWS
    padpos = jnp.stack([pend, padded - cnt], axis=1).reshape(2 * N_CLASSES)
    tail = jnp.stack([pend[-1], (p_rows - pend[-1]) // ZERO_ROWS])
    padpos = jnp.pad(jnp.concatenate([padpos, tail]), (0, 128 - 2 * N_CLASSES - 2)).astype(i32)[None, :]

    xs = _scatter_rows(xn, pos.reshape(n // ts, 1, ts), padpos, p_rows, ts)
    ys = _experts(xs, tile_ea.astype(i32), tile_eb.astype(i32), n_used, lw, te)
    return _combine_ple(ys, pos.reshape(n // tc, 1, tc), h1, p_l, lw, tc)


_B_HEAD_ORDER = np.array([0, 4, 1, 5, 2, 6, 3, 7])


def _block_mean_matrix(width, block):
    idx = np.arange(width)
    return ((idx[:, None] // block) == (idx[None, :] // block)).astype(np.float32) / block


def _rotate_half_matrix(width, block):
    half = block // 2
    m = np.zeros((width, width), np.float32)
    for j in range(width):
        if j % block < half:
            m[j + half, j] = -1.0
        else:
            m[j - half, j] = 1.0
    return m


def _constants():
    bf16 = jnp.bfloat16
    return {
        "m32": jnp.asarray(_block_mean_matrix(256, A_DH), bf16),
        "m64": jnp.asarray(_block_mean_matrix(256, HEAD_DIM), bf16),
        "rot": jnp.asarray(_rotate_half_matrix(256, A_DH), bf16),
    }


def _rope_tables(seq):
    def table(pos, dim):
        inv = 1.0 / (ROPE_THETA ** (jnp.arange(0, dim, 2, dtype=jnp.float32) / dim))
        ang = pos.astype(jnp.float32)[:, None] * inv[None, :]
        return jnp.cos(ang), jnp.sin(ang)

    rows = seq // GRID_W
    pos = jnp.arange(seq)
    row = jnp.repeat(jnp.arange(rows), GRID_W)
    col = jnp.tile(jnp.arange(GRID_W), rows)
    c1, s1 = table(pos, A_DH)
    cr, sr = table(row, HEAD_DIM // 2)
    cc, sc = table(col, HEAD_DIM // 2)
    return {
        "cosa": jnp.tile(jnp.concatenate([c1, c1], axis=1), (1, 4)),
        "sina": jnp.tile(jnp.concatenate([s1, s1], axis=1), (1, 4)),
        "cosb": jnp.tile(jnp.concatenate([cr, cr, cc, cc], axis=1), (1, 2)),
        "sinb": jnp.tile(jnp.concatenate([sr, sr, sc, sc], axis=1), (1, 2)),
    }


def _prep_layer(l, g_mix, w_in, a_q_norm, a_k_norm, a_lambda, a_subln, b_q_norm, b_k_norm, c_gate_w,
                c_gate_b, c_out_norm, w_out, g_ffn, w_rg, w_re, w_e1, w_e3, w_e2, g_ple, w_pg, w_pp):
    f32 = jnp.float32
    bf16 = jnp.bfloat16
    aq, ak, av = w_in[:, 0:256], w_in[:, 256:512], w_in[:, 512:768]
    bq, bk, bv = w_in[:, 768:1280], w_in[:, 1280:1408], w_in[:, 1408:1536]
    crest = w_in[:, 1536:2336]
    bq = bq.reshape(D_MODEL, B_HEADS, HEAD_DIM)[:, _B_HEAD_ORDER, :].reshape(D_MODEL, 512)
    w_main = jnp.concatenate(
        [aq, ak, bq, bk, crest, jnp.zeros((D_MODEL, MAIN_COLS - QK_COLS - 800), f32)], axis=1)
    w_vt = jnp.concatenate([av, bv], axis=1).T

    gn = jnp.concatenate([
        jnp.tile(a_q_norm, 8) * (A_DH ** -0.5 * LOG2E),
        jnp.tile(a_k_norm, 8),
        jnp.tile(b_q_norm, 8) * (HEAD_DIM ** -0.5 * LOG2E),
        jnp.tile(b_k_norm, 2),
    ])[None, :]

    gw = jnp.zeros((128, 256), f32)
    gw = gw.at[0:C_GATE_RANK, 0:128].set(c_gate_w[0])
    gw = gw.at[C_GATE_RANK:2 * C_GATE_RANK, 128:256].set(c_gate_w[1])
    gb = jnp.concatenate([c_gate_b[0], c_gate_b[1]])[None, :]

    gq_a = jnp.max(jnp.abs(a_q_norm)) * (A_DH ** -0.5 * LOG2E)
    gq_b = jnp.max(jnp.abs(b_q_norm)) * (HEAD_DIM ** -0.5 * LOG2E)
    bound = 1.05 * jnp.maximum(A_DH * gq_a * jnp.max(jnp.abs(a_k_norm)),
                               HEAD_DIM * gq_b * jnp.max(jnp.abs(b_k_norm)))
    bounded = bound <= SCORE_BOUND

    lam_init = 0.8 - 0.6 * math.exp(-0.3 * l)
    lf = a_lambda.astype(f32)
    lam = jnp.exp(jnp.sum(lf[0] * lf[1])) - jnp.exp(jnp.sum(lf[2] * lf[3])) + lam_init

    wo_b = w_out[256:768].reshape(B_HEADS, HEAD_DIM, D_MODEL)[_B_HEAD_ORDER].reshape(512, D_MODEL)
    w_out_p = jnp.concatenate([w_out[0:256], wo_b, w_out[768:1024]], axis=0)

    wr = jnp.zeros((D_MODEL, ROUTE_LANES), f32)
    wr = wr.at[:, 0:N_GROUPS].set(w_rg)
    wr = wr.at[:, ROUTE_OFF:ROUTE_OFF + N_EXPERTS].set(
        jnp.transpose(w_re, (1, 0, 2)).reshape(D_MODEL, N_EXPERTS))
    wr_hi = wr.astype(bf16)
    wr_lo = (wr - wr_hi.astype(f32)).astype(bf16)

    return {
        "g_mix": g_mix[None, :], "w_main": w_main.astype(bf16), "w_vt": w_vt.astype(bf16), "gn": gn,
        "gw": gw.astype(bf16), "gb": gb, "lam": jnp.reshape(lam, (1, 1)), "bounded": bounded,
        "gsub": (a_subln * (1.0 - lam_init))[:, None],
        "gc": jnp.tile(c_out_norm, C_HEADS)[None, :], "w_out": w_out_p.astype(bf16),
        "g_ffn": g_ffn[None, :], "wr_t": jnp.concatenate([wr_hi.T, wr_lo.T], axis=0),
        "w1": w_e1.astype(bf16), "w3": w_e3.astype(bf16), "w2": w_e2.astype(bf16),
        "g_ple": g_ple[None, :], "w_pg": w_pg.astype(bf16), "w_pp": w_pp.astype(bf16),
    }


def _tile(limit, total):
    t = min(limit, total)
    assert total % t == 0
    return t


def _trunk(x, p, layers, consts):
    batch, seq, _ = x.shape
    n = batch * seq
    tabs = _rope_tables(seq)
    tm = _tile(512, seq)
    ts = _tile(1024, seq)
    tc = _tile(512, seq)
    te = 512
    h = x.reshape(n, D_MODEL)
    for l in range(DEPTH):
        lw = layers[l]
        qa, ka, qb, kb, vt, cqk, la, cv, cgate = _in_proj(h, lw, tabs, consts, seq, tm)
        oa = lax.cond(lw["bounded"],
                      lambda *a: _diff_attn(*a, batch, seq, True),
                      lambda *a: _diff_attn(*a, batch, seq, False),
                      qa, ka, vt, lw["lam"], lw["gsub"])
        ob = lax.cond(lw["bounded"],
                      lambda *a: _gqa_attn(*a, batch, seq, True),
                      lambda *a: _gqa_attn(*a, batch, seq, False),
                      qb, kb, vt)
        gla_ok = GLA_CHUNK * jnp.max(-la) <= GLA_DECAY_BOUND
        o_f, o_b = lax.cond(
            gla_ok,
            lambda *a: (_gla(*a, batch, seq, False, True), _gla(*a, batch, seq, True, True)),
            lambda *a: (_gla(*a, batch, seq, False, False), _gla(*a, batch, seq, True, False)),
            cqk, la, cv)
        h1, xn, route, counts = _out_proj(h, oa, ob, o_f, o_b, cgate, lw, consts, tm)
        h = _moe_ple(h1, xn, route, counts, p[l].reshape(n, PLE_DIM), lw, ts, tc, te)
    return h.reshape(batch, seq, D_MODEL)


def kernel(x_prompt, x_sample, p_prompt, p_sample, g_mix, w_in, a_q_norm, a_k_norm, a_lambda, a_subln, b_q_norm, b_k_norm, c_gate_w, c_gate_b, c_out_norm, w_out, g_ffn, w_router_group, w_router_expert, w_exp_gate, w_exp_up, w_exp_down, g_ple, w_ple_gate, w_ple_proj):
    consts = _constants()
    layers = [
        _prep_layer(l, g_mix[l], w_in[l], a_q_norm[l], a_k_norm[l], a_lambda[l], a_subln[l], b_q_norm[l],
                    b_k_norm[l], c_gate_w[l], c_gate_b[l], c_out_norm[l], w_out[l], g_ffn[l],
                    w_router_group[l], w_router_expert[l], w_exp_gate[l], w_exp_up[l], w_exp_down[l],
                    g_ple[l], w_ple_gate[l], w_ple_proj[l])
        for l in range(DEPTH)
    ]
    y_prompt = _trunk(x_prompt, p_prompt, layers, consts)
    y_sample = _trunk(x_sample, p_sample, layers, consts)
    return (y_prompt, y_sample)
```

```python
import functools
import math

import jax
import jax.numpy as jnp
import numpy as np
from jax import lax
from jax.experimental import pallas as pl
from jax.experimental.pallas import tpu as pltpu

D_MODEL = 1024
DEPTH = 4
HEAD_DIM = 64
A_HEADS = 4
A_DH = 32
B_HEADS = 8
B_KV_HEADS = 2
C_HEADS = 4
C_DK = 32
C_DV = 64
C_GATE_RANK = 16
GATE_TAU = 16.0
GLA_CHUNK = 64
GLA_SUB = 16
GRID_W = 64
ROPE_THETA = 10000.0
PLE_DIM = 256
N_GROUPS = 4
EXPERTS_PER_GROUP = 4
N_EXPERTS = 16
D_EXPERT = 512
EPS = 1e-6
LOG2E = 1.4426950408889634

VMEM_LIMIT_BYTES = 52 * 1024 * 1024
NEG_INF = float("-inf")

QK_COLS = 1152
MAIN_COLS = 2048
VT_HEAD_ROWS = 80
V_ROWS = 384
VT_ROWS = 6 * VT_HEAD_ROWS
TQ_BOUNDED = 512
TQ_SHIFTED = 256
SCORE_BOUND = 60.0
GLA_DECAY_BOUND = 80.0
TB_BOUNDED = 1024
TB_EXACT = 256
ZERO_ROWS = 256
SUBLANES = 8
ROUTE_LANES = 128
ROUTE_OFF = 4
PAIRS_PER_GROUP = 6
N_CLASSES = N_GROUPS * PAIRS_PER_GROUP
PAIR_A = (0, 0, 0, 1, 1, 2)
PAIR_B = (1, 2, 3, 2, 3, 3)
XROW = D_MODEL + 128


def _cparams(sem):
    return pltpu.CompilerParams(dimension_semantics=sem, vmem_limit_bytes=VMEM_LIMIT_BYTES)


def _bf(x):
    return x.astype(jnp.bfloat16)


def _dot(a, b):
    return jnp.dot(a, b, preferred_element_type=jnp.float32)


def _dot_nt(a, b):
    return lax.dot_general(a, b, (((1,), (1,)), ((), ())), preferred_element_type=jnp.float32)


def _dot_tn(a, b):
    return lax.dot_general(a, b, (((0,), (0,)), ((), ())), preferred_element_type=jnp.float32)


def _iota(shape, dim):
    return lax.broadcasted_iota(jnp.int32, shape, dim)


def _in_proj_kernel(h_ref, gmix_ref, w_ref, wvt_ref, gn_ref, cosa_ref, sina_ref, cosb_ref, sinb_ref,
                    m32_ref, m64_ref, rot_ref, gw_ref, gb_ref,
                    qa_ref, ka_ref, qb_ref, kb_ref, vt_ref, cqk_ref, la_ref, cv_ref, cg_ref):
    x = h_ref[...]
    a = x * lax.rsqrt(jnp.mean(x * x, axis=-1, keepdims=True) + EPS) * gmix_ref[...]
    ab = _bf(a)
    proj = _dot(ab, w_ref[...])
    vt = _bf(_dot_nt(wvt_ref[...], ab))
    ones = jnp.ones((VT_HEAD_ROWS - HEAD_DIM, vt.shape[1]), jnp.bfloat16)
    for hh in range(VT_ROWS // VT_HEAD_ROWS):
        vt_ref[VT_HEAD_ROWS * hh:VT_HEAD_ROWS * hh + HEAD_DIM, :] = vt[HEAD_DIM * hh:HEAD_DIM * (hh + 1), :]
        vt_ref[VT_HEAD_ROWS * hh + HEAD_DIM:VT_HEAD_ROWS * (hh + 1), :] = ones

    cosa = cosa_ref[...]
    sina = sina_ref[...]
    cosb = cosb_ref[...]
    sinb = sinb_ref[...]
    cosa2 = jnp.concatenate([cosa, cosa], axis=1)
    sina2 = jnp.concatenate([sina, sina], axis=1)
    cosb2 = jnp.concatenate([cosb, cosb], axis=1)
    sinb2 = jnp.concatenate([sinb, sinb], axis=1)

    def qk_slab(off, width, mean_mat, cos, sin, out_ref, out_off):
        xs = proj[:, off:off + width]
        ms = _dot(_bf(xs * xs), mean_mat)
        xn = xs * lax.rsqrt(ms + EPS) * gn_ref[:, off:off + width]
        rot = _dot(_bf(xn), rot_ref[0:width, 0:width])
        out_ref[:, out_off:out_off + width] = _bf(xn * cos + rot * sin)

    m32 = m32_ref[...]
    m64 = m64_ref[...]
    qk_slab(0, 256, m32, cosa2, sina2, qa_ref, 0)
    qk_slab(256, 256, m32, cosa2, sina2, ka_ref, 0)
    qk_slab(512, 256, m64, cosb2, sinb2, qb_ref, 0)
    qk_slab(768, 256, m64, cosb2, sinb2, qb_ref, 256)
    qk_slab(1024, 128, m64_ref[0:128, 0:128], cosb, sinb, kb_ref, 0)

    c0 = QK_COLS
    cqk_ref[:, 0:128] = proj[:, c0:c0 + 128] * (C_DK ** -0.5)
    cqk_ref[:, 128:256] = proj[:, c0 + 128:c0 + 256]
    cv_ref[...] = proj[:, c0 + 256:c0 + 512]
    cg = proj[:, c0 + 512:c0 + 768]
    cg_ref[...] = cg * (1.0 / (1.0 + jnp.exp(-cg)))
    clr = proj[:, c0 + 768:c0 + 896]
    z = _dot(_bf(clr), gw_ref[...]) + gb_ref[...]
    la_ref[...] = (jnp.minimum(z, 0.0) - jnp.log(1.0 + jnp.exp(-jnp.abs(z)))) * (1.0 / GATE_TAU)


def _in_proj(h, lw, tabs, consts, seq, tm):
    n = h.shape[0]
    nt = n // tm
    spb = seq // tm

    def row(i):
        return (i, 0)

    def full(i):
        return (0, 0)

    def tab(i):
        return (i % spb, 0)

    f32 = jnp.float32
    bf16 = jnp.bfloat16
    out_shape = (
        jax.ShapeDtypeStruct((n, 256), bf16),
        jax.ShapeDtypeStruct((n, 256), bf16),
        jax.ShapeDtypeStruct((n, 512), bf16),
        jax.ShapeDtypeStruct((n, 128), bf16),
        jax.ShapeDtypeStruct((VT_ROWS, n), bf16),
        jax.ShapeDtypeStruct((n, 256), f32),
        jax.ShapeDtypeStruct((n, 256), f32),
        jax.ShapeDtypeStruct((n, 256), f32),
        jax.ShapeDtypeStruct((n, 256), f32),
    )
    in_specs = [
        pl.BlockSpec((tm, D_MODEL), row),
        pl.BlockSpec((1, D_MODEL), full),
        pl.BlockSpec((D_MODEL, MAIN_COLS), full),
        pl.BlockSpec((V_ROWS, D_MODEL), full),
        pl.BlockSpec((1, QK_COLS), full),
        pl.BlockSpec((tm, 128), tab),
        pl.BlockSpec((tm, 128), tab),
        pl.BlockSpec((tm, 128), tab),
        pl.BlockSpec((tm, 128), tab),
        pl.BlockSpec((256, 256), full),
        pl.BlockSpec((256, 256), full),
        pl.BlockSpec((256, 256), full),
        pl.BlockSpec((128, 256), full),
        pl.BlockSpec((1, 256), full),
    ]
    out_specs = (
        pl.BlockSpec((tm, 256), row),
        pl.BlockSpec((tm, 256), row),
        pl.BlockSpec((tm, 512), row),
        pl.BlockSpec((tm, 128), row),
        pl.BlockSpec((VT_ROWS, tm), lambda i: (0, i)),
        pl.BlockSpec((tm, 256), row),
        pl.BlockSpec((tm, 256), row),
        pl.BlockSpec((tm, 256), row),
        pl.BlockSpec((tm, 256), row),
    )
    return pl.pallas_call(
        _in_proj_kernel, out_shape=out_shape, grid=(nt,), in_specs=in_specs, out_specs=out_specs,
        compiler_params=_cparams(("parallel",)), name="in_proj",
    )(h, lw["g_mix"], lw["w_main"], lw["w_vt"], lw["gn"], tabs["cosa"], tabs["sina"], tabs["cosb"],
      tabs["sinb"], consts["m32"], consts["m64"], consts["rot"], lw["gw"], lw["gb"])


def _softmax_pv(k, qm, vt, bounded):
    st = _dot_nt(k, qm)
    if bounded:
        p = jnp.exp2(st)
    else:
        p = jnp.exp2(st - jnp.max(st, axis=0, keepdims=True))
    ot = _dot(vt, _bf(p))
    return ot[0:HEAD_DIM, :] / ot[HEAD_DIM:HEAD_DIM + 1, :]


def _diff_attn_kernel(q_ref, k_ref, vt_ref, lam_ref, gsub_ref, o_ref, *, bounded):
    q = q_ref[...]
    k = k_ref[...]
    lane = _iota(q.shape, 1)
    zero = jnp.zeros_like(q)
    lam = lam_ref[...]
    gsub = gsub_ref[...]
    outs = []
    for h in range(A_HEADS):
        vt = vt_ref[VT_HEAD_ROWS * h:VT_HEAD_ROWS * (h + 1), :]
        comp = []
        for c in range(2):
            lo = A_DH * (2 * h + c)
            qm = jnp.where((lane >= lo) & (lane < lo + A_DH), q, zero)
            comp.append(_softmax_pv(k, qm, vt, bounded))
        od = comp[0] - lam * comp[1]
        r = lax.rsqrt(jnp.mean(od * od, axis=0, keepdims=True) + EPS)
        outs.append(od * r * gsub)
    ot = jnp.concatenate(outs, axis=0)
    o_ref[...] = _bf(ot.T)


def _gqa_attn_kernel(q_ref, k_ref, vt_ref, o_ref, *, bounded):
    k = k_ref[...]
    outs = []
    for j in range(B_HEADS // B_KV_HEADS):
        q = q_ref[:, 128 * j:128 * (j + 1)]
        lane = _iota(q.shape, 1)
        zero = jnp.zeros_like(q)
        for g in range(B_KV_HEADS):
            qm = jnp.where((lane >= HEAD_DIM * g) & (lane < HEAD_DIM * (g + 1)), q, zero)
            vt = vt_ref[VT_HEAD_ROWS * g:VT_HEAD_ROWS * (g + 1), :]
            outs.append(_softmax_pv(k, qm, vt, bounded))
    ot = jnp.concatenate(outs, axis=0)
    o_ref[...] = _bf(ot.T)


def _diff_attn(qa, ka, vt, lam, gsub, batch, seq, bounded):
    n = qa.shape[0]
    tq = _tile(TQ_BOUNDED if bounded else TQ_SHIFTED, seq)
    nq = seq // tq
    return pl.pallas_call(
        functools.partial(_diff_attn_kernel, bounded=bounded),
        out_shape=jax.ShapeDtypeStruct((n, 256), jnp.bfloat16),
        grid=(batch, nq),
        in_specs=[
            pl.BlockSpec((tq, 256), lambda b, i: (b * nq + i, 0)),
            pl.BlockSpec((seq, 256), lambda b, i: (b, 0)),
            pl.BlockSpec((A_HEADS * VT_HEAD_ROWS, seq), lambda b, i: (0, b)),
            pl.BlockSpec((1, 1), lambda b, i: (0, 0)),
            pl.BlockSpec((HEAD_DIM, 1), lambda b, i: (0, 0)),
        ],
        out_specs=pl.BlockSpec((tq, 256), lambda b, i: (b * nq + i, 0)),
        compiler_params=_cparams(("parallel", "parallel")),
        name="diff_attn" if bounded else "diff_attn_shifted",
    )(qa, ka, vt, lam, gsub)


def _gqa_attn(qb, kb, vt, batch, seq, bounded):
    n = qb.shape[0]
    tq = _tile(TQ_BOUNDED if bounded else TQ_SHIFTED, seq)
    nq = seq // tq
    return pl.pallas_call(
        functools.partial(_gqa_attn_kernel, bounded=bounded),
        out_shape=jax.ShapeDtypeStruct((n, 512), jnp.bfloat16),
        grid=(batch, nq),
        in_specs=[
            pl.BlockSpec((tq, 512), lambda b, i: (b * nq + i, 0)),
            pl.BlockSpec((seq, 128), lambda b, i: (b, 0)),
            pl.BlockSpec((B_KV_HEADS * VT_HEAD_ROWS, seq), lambda b, i: (A_HEADS // B_KV_HEADS, b)),
        ],
        out_specs=pl.BlockSpec((tq, 512), lambda b, i: (b * nq + i, 0)),
        compiler_params=_cparams(("parallel", "parallel")),
        name="gqa_attn" if bounded else "gqa_attn_shifted",
    )(qb, kb, vt)


def _split3(x):
    hi = _bf(x)
    r1 = x - hi.astype(jnp.float32)
    mid = _bf(r1)
    lo = _bf(r1 - mid.astype(jnp.float32))
    return hi, mid, lo


def _gla_kernel(qk_ref, g_ref, v_ref, o_ref, st_ref, *, nchunk, rev):
    C = GLA_CHUNK
    SB = GLA_SUB
    nsub = C // SB

    @pl.when(pl.program_id(1) == 0)
    def _():
        st_ref[...] = jnp.zeros_like(st_ref)

    r64 = _iota((C, C), 0)
    c64 = _iota((C, C), 1)
    tri = _bf(jnp.where((c64 >= r64) if rev else (c64 <= r64), 1.0, 0.0))
    hm_k = (_iota((4 * C, 128), 0) >> 6) == (_iota((4 * C, 128), 1) >> 5)
    hm_v = (_iota((4 * C, 256), 0) >> 6) == (_iota((4 * C, 256), 1) >> 6)
    ones_exp = _bf(jnp.where((_iota((128, 256), 0) >> 5) == (_iota((128, 256), 1) >> 6), 1.0, 0.0))
    rowc = _iota((C, 128), 0)
    rows = _iota((SB, 128), 0)

    order = range(nchunk - 1, -1, -1) if rev else range(nchunk)
    for ci in order:
        r0 = ci * C
        q = qk_ref[r0:r0 + C, 0:128]
        k = qk_ref[r0:r0 + C, 128:256]
        g = g_ref[r0:r0 + C, :]
        v = v_ref[r0:r0 + C, :]
        g1, g2, g3 = _split3(g)
        b = _dot(tri, g1) + _dot(tri, g2) + _dot(tri, g3)

        st = st_ref[...]
        o = _dot_nt(_bf(q * jnp.exp(b)), _bf(st))
        bl = b[0:1, :] if rev else b[C - 1:C, :]
        ke = k * jnp.exp(bl - b)
        dst = _dot_tn(_bf(v), _bf(ke))
        st_ref[...] = st * jnp.exp(bl) + jnp.where(hm_k, dst, 0.0)

        att = jnp.zeros((C, 4 * C), jnp.float32)
        for blk in (range(0, nsub - 1) if rev else range(1, nsub)):
            if rev:
                ref = b[SB * blk + SB - 1:SB * blk + SB, :]
                kmask = rowc >= SB * (blk + 1)
            else:
                ref = b[SB * blk:SB * blk + 1, :]
                kmask = rowc < SB * blk
            qmask = (rowc >= SB * blk) & (rowc < SB * (blk + 1))
            qd = q * jnp.exp(jnp.where(qmask, b - ref, NEG_INF))
            kd = k * jnp.exp(jnp.where(kmask, ref - b, NEG_INF))
            kexp = jnp.where(hm_k, jnp.concatenate([kd] * 4, axis=0), 0.0)
            att = att + _dot_nt(_bf(qd), _bf(kexp))
        vexp = jnp.where(hm_v, jnp.concatenate([v] * 4, axis=0), 0.0)
        o = o + _dot(_bf(att), _bf(vexp))

        diag = []
        for blk in range(nsub):
            s0 = SB * blk
            bs = b[s0:s0 + SB, :]
            qs = q[s0:s0 + SB, :]
            ks = k[s0:s0 + SB, :]
            vs = v[s0:s0 + SB, :]
            ws = []
            for j in range(SB):
                keep = (rows <= j) if rev else (rows >= j)
                e = jnp.exp(jnp.where(keep, bs - bs[j:j + 1, :], NEG_INF))
                ws.append(e * qs * ks[j:j + 1, :])
            p = _dot(_bf(jnp.concatenate(ws, axis=0)), ones_exp)
            od = p[0:SB, :] * vs[0:1, :]
            for j in range(1, SB):
                od = od + p[SB * j:SB * (j + 1), :] * vs[j:j + 1, :]
            diag.append(od)
        o_ref[r0:r0 + C, :] = o + jnp.concatenate(diag, axis=0)


def _gla_bounded_kernel(qk_ref, g_ref, v_ref, o_ref, st_ref, *, nchunk, rev):
    C = GLA_CHUNK

    @pl.when(pl.program_id(1) == 0)
    def _():
        st_ref[...] = jnp.zeros_like(st_ref)

    r64 = _iota((C, C), 0)
    c64 = _iota((C, C), 1)
    tri = _bf(jnp.where((c64 >= r64) if rev else (c64 <= r64), 1.0, 0.0))
    hm_k = (_iota((4 * C, 128), 0) >> 6) == (_iota((4 * C, 128), 1) >> 5)
    hm_v = (_iota((4 * C, 256), 0) >> 6) == (_iota((4 * C, 256), 1) >> 6)
    qpos = _iota((C, 4 * C), 0)
    kpos = _iota((C, 4 * C), 1) & (C - 1)
    keep = (kpos >= qpos) if rev else (kpos <= qpos)

    order = range(nchunk - 1, -1, -1) if rev else range(nchunk)
    rng = [(ci * C, (ci + 1) * C) for ci in order]
    bs = []
    for lo, hi in rng:
        g1, g2, g3 = _split3(g_ref[lo:hi, :])
        bs.append(_dot(tri, g1) + _dot(tri, g2) + _dot(tri, g3))
    qes = [_bf(qk_ref[lo:hi, 0:128] * jnp.exp(b)) for (lo, hi), b in zip(rng, bs)]
    kds = [qk_ref[lo:hi, 128:256] * jnp.exp(-b) for (lo, hi), b in zip(rng, bs)]
    ebls = [jnp.exp(b[0:1, :] if rev else b[C - 1:C, :]) for b in bs]
    vbs = [_bf(v_ref[lo:hi, :]) for lo, hi in rng]
    atts = [jnp.where(keep, _dot_nt(qe, _bf(jnp.where(hm_k, jnp.concatenate([kd] * 4, axis=0), 0.0))), 0.0)
            for qe, kd in zip(qes, kds)]
    zero = jnp.zeros((), jnp.bfloat16)
    intras = [_dot(_bf(att), jnp.where(hm_v, jnp.concatenate([vb] * 4, axis=0), zero))
              for att, vb in zip(atts, vbs)]
    dsts = [jnp.where(hm_k, _dot_tn(vb, _bf(kd * ebl)), 0.0) for vb, kd, ebl in zip(vbs, kds, ebls)]

    st = st_ref[...]
    for (lo, hi), qe, o_intra, ebl, dst in zip(rng, qes, intras, ebls, dsts):
        o_ref[lo:hi, :] = o_intra + _dot_nt(qe, _bf(st))
        st = st * ebl + dst
    st_ref[...] = st


def _gla(cqk, la, cv, batch, seq, rev, bounded):
    n = cqk.shape[0]
    tb = _tile(TB_BOUNDED if bounded else TB_EXACT, seq)
    nb = seq // tb

    def blk(b, i):
        return (b * nb + (nb - 1 - i if rev else i), 0)

    def gblk(b, i):
        return (b * nb + (nb - 1 - i if rev else i), 1 if rev else 0)

    return pl.pallas_call(
        functools.partial(_gla_bounded_kernel if bounded else _gla_kernel, nchunk=tb // GLA_CHUNK, rev=rev),
        out_shape=jax.ShapeDtypeStruct((n, 256), jnp.float32),
        grid=(batch, nb),
        in_specs=[
            pl.BlockSpec((tb, 256), blk),
            pl.BlockSpec((tb, 128), gblk),
            pl.BlockSpec((tb, 256), blk),
        ],
        out_specs=pl.BlockSpec((tb, 256), blk),
        scratch_shapes=[pltpu.VMEM((256, 128), jnp.float32)],
        compiler_params=_cparams(("parallel", "arbitrary")),
        name=("gla_bwd" if rev else "gla_fwd") + ("" if bounded else "_exact"),
    )(cqk, la, cv)


def _out_proj_kernel(h_ref, oa_ref, ob_ref, of_ref, obw_ref, cg_ref, gc_ref, m64_ref, wo_ref, gffn_ref,
                     wrt_ref, h1_ref, xn_ref, route_ref, cnt_ref):
    @pl.when(pl.program_id(0) == 0)
    def _():
        cnt_ref[...] = jnp.zeros_like(cnt_ref)

    oc = of_ref[...] + obw_ref[...]
    ms = _dot(_bf(oc * oc), m64_ref[...])
    oc = oc * lax.rsqrt(ms + EPS) * gc_ref[...] * cg_ref[...]
    h1 = (h_ref[...] + _dot(oa_ref[...], wo_ref[0:256, :]) + _dot(ob_ref[...], wo_ref[256:768, :])
          + _dot(_bf(oc), wo_ref[768:1024, :]))
    h1_ref[...] = h1
    xn = h1 * lax.rsqrt(jnp.mean(h1 * h1, axis=-1, keepdims=True) + EPS) * gffn_ref[...]
    xn_ref[:, 0:D_MODEL] = xn
    xh = _bf(xn)
    xl = _bf(xn - xh.astype(jnp.float32))
    lgt = _dot_nt(wrt_ref[...], xh)
    lgt = lgt[0:ROUTE_LANES, :] + lgt[ROUTE_LANES:2 * ROUTE_LANES, :] + _dot_nt(wrt_ref[0:ROUTE_LANES, :], xl)
    lg = lgt.T

    lane = _iota(lg.shape, 1)
    big = jnp.int32(1 << 20)
    is_g = lane < N_GROUPS
    mg = jnp.max(jnp.where(is_g, lg, NEG_INF), axis=-1, keepdims=True)
    sg = jnp.sum(jnp.where(is_g, jnp.exp(lg - mg), 0.0), axis=-1, keepdims=True)
    p_top = 1.0 / sg
    g_top = jnp.min(jnp.where(is_g & (lg == mg), lane, big), axis=-1, keepdims=True)
    e_lo = ROUTE_OFF + EXPERTS_PER_GROUP * g_top
    sel = (lane >= e_lo) & (lane < e_lo + EXPERTS_PER_GROUP)
    ev0 = jnp.max(jnp.where(sel, lg, NEG_INF), axis=-1, keepdims=True)
    i0 = jnp.min(jnp.where(sel & (lg == ev0), lane, big), axis=-1, keepdims=True)
    sel1 = sel & (lane != i0)
    ev1 = jnp.max(jnp.where(sel1, lg, NEG_INF), axis=-1, keepdims=True)
    i1 = jnp.min(jnp.where(sel1 & (lg == ev1), lane, big), axis=-1, keepdims=True)
    t = jnp.exp(ev1 - ev0)
    pe0 = p_top / (1.0 + t)
    pe1 = p_top * t / (1.0 + t)

    l0 = i0 - ROUTE_OFF - EXPERTS_PER_GROUP * g_top
    l1 = i1 - ROUTE_OFF - EXPERTS_PER_GROUP * g_top
    la = jnp.minimum(l0, l1)
    lb = jnp.maximum(l0, l1)
    cls = PAIRS_PER_GROUP * g_top + ((la * (7 - la)) >> 1) + (lb - la - 1)
    wa = jnp.where(l0 < l1, pe0, pe1)
    wb = jnp.where(l0 < l1, pe1, pe0)

    tm = lg.shape[0]
    picked = _bf(jnp.where(lane == cls, 1.0, 0.0))
    strict_lower = _bf(jnp.where(_iota((tm, tm), 1) < _iota((tm, tm), 0), 1.0, 0.0))
    prefix = _dot(strict_lower, picked) + cnt_ref[...]
    rank = jnp.sum(jnp.where(lane == cls, prefix, 0.0), axis=-1, keepdims=True)
    cnt_ref[...] += jnp.sum(picked.astype(jnp.float32), axis=0, keepdims=True)

    wrec = jnp.where(lane == 0, wa, jnp.where(lane == 1, wb, 0.0))
    xn_ref[:, D_MODEL:XROW] = wrec
    route_ref[...] = jnp.where(lane == 0, cls.astype(jnp.float32), jnp.where(lane == 1, rank, 0.0))


def _out_proj(h, oa, ob, o_f, o_b, cgate, lw, consts, tm):
    n = h.shape[0]

    def row(i):
        return (i, 0)

    def full(i):
        return (0, 0)

    return pl.pallas_call(
        _out_proj_kernel,
        out_shape=(jax.ShapeDtypeStruct((n, D_MODEL), jnp.float32),
                   jax.ShapeDtypeStruct((n, XROW), jnp.float32),
                   jax.ShapeDtypeStruct((n, ROUTE_LANES), jnp.float32),
                   jax.ShapeDtypeStruct((1, ROUTE_LANES), jnp.float32)),
        grid=(n // tm,),
        in_specs=[
            pl.BlockSpec((tm, D_MODEL), row),
            pl.BlockSpec((tm, 256), row),
            pl.BlockSpec((tm, 512), row),
            pl.BlockSpec((tm, 256), row),
            pl.BlockSpec((tm, 256), row),
            pl.BlockSpec((tm, 256), row),
            pl.BlockSpec((1, 256), full),
            pl.BlockSpec((256, 256), full),
            pl.BlockSpec((D_MODEL, D_MODEL), full),
            pl.BlockSpec((1, D_MODEL), full),
            pl.BlockSpec((2 * ROUTE_LANES, D_MODEL), full),
        ],
        out_specs=(pl.BlockSpec((tm, D_MODEL), row), pl.BlockSpec((tm, XROW), row),
                   pl.BlockSpec((tm, ROUTE_LANES), row), pl.BlockSpec((1, ROUTE_LANES), full)),
        compiler_params=_cparams(("arbitrary",)), name="out_proj",
    )(h, oa, ob, o_f, o_b, cgate, lw["gc"], consts["m64"], lw["w_out"], lw["g_ffn"], lw["wr_t"])


def _row_copy(src, src_row, dst, dst_row, sem):
    return pltpu.make_async_copy(src.at[pl.ds(src_row, 1), :], dst.at[pl.ds(dst_row, 1), :], sem)


def _scatter_kernel(pos_ref, pad_ref, x_ref, xs_ref, pos_smem, pad_smem, zrow_ref, sem_idx, sem_rows):
    ts = x_ref.shape[0]
    cp = pltpu.make_async_copy(pos_ref, pos_smem, sem_idx)
    cp.start()
    cp.wait()

    def issue(r2, carry):
        _row_copy(x_ref, 2 * r2, xs_ref, pos_smem[0, 0, 2 * r2], sem_rows).start(priority=0)
        _row_copy(x_ref, 2 * r2 + 1, xs_ref, pos_smem[0, 0, 2 * r2 + 1], sem_rows).start(priority=1)
        return carry

    lax.fori_loop(0, ts // 2, issue, 0, unroll=8)

    def drain(r, carry):
        _row_copy(x_ref, 0, xs_ref, 0, sem_rows).wait()
        return carry

    lax.fori_loop(0, ts, drain, 0, unroll=8)

    @pl.when(pl.program_id(0) == pl.num_programs(0) - 1)
    def _():
        zrow_ref[...] = jnp.zeros_like(zrow_ref)
        cz = pltpu.make_async_copy(pad_ref, pad_smem, sem_idx)
        cz.start()
        cz.wait()

        def fill(start_wait):
            for c in range(N_CLASSES):
                end = pad_smem[0, 2 * c]
                gap = pad_smem[0, 2 * c + 1]
                for j in range(SUBLANES - 1):
                    @pl.when(j < (gap & (SUBLANES - 1)))
                    def _(j=j, end=end, gap=gap):
                        cp = _row_copy(zrow_ref, 0, xs_ref, end - gap + j, sem_rows)
                        if start_wait:
                            cp.start()
                        else:
                            cp.wait()
                filled = 0
                run = ZERO_ROWS
                while run >= SUBLANES:
                    @pl.when((gap & run) != 0)
                    def _(end=end, filled=filled, run=run):
                        start = pl.multiple_of(end - filled - run, run)
                        cp = pltpu.make_async_copy(zrow_ref.at[pl.ds(0, run), :],
                                                   xs_ref.at[pl.ds(start, run), :], sem_rows)
                        if start_wait:
                            cp.start()
                        else:
                            cp.wait()
                    filled = filled + (gap & run)
                    run //= 2
            tail = pad_smem[0, 2 * N_CLASSES]
            ntail = pad_smem[0, 2 * N_CLASSES + 1]
            for j in range(2 * N_CLASSES):
                @pl.when(j < ntail)
                def _(j=j):
                    start = pl.multiple_of(tail + ZERO_ROWS * j, ZERO_ROWS)
                    cp = pltpu.make_async_copy(zrow_ref, xs_ref.at[pl.ds(start, ZERO_ROWS), :], sem_rows)
                    if start_wait:
                        cp.start()
                    else:
                        cp.wait()

        fill(True)
        fill(False)


def _scatter_rows(xn, pos, padpos, p_rows, ts):
    n = xn.shape[0]
    npad = padpos.shape[1]
    return pl.pallas_call(
        _scatter_kernel,
        out_shape=jax.ShapeDtypeStruct((p_rows, XROW), jnp.float32),
        grid=(n // ts,),
        in_specs=[
            pl.BlockSpec((1, 1, ts), lambda i: (i, 0, 0)),
            pl.BlockSpec((1, npad), lambda i: (0, 0)),
            pl.BlockSpec((ts, XROW), lambda i: (i, 0)),
        ],
        out_specs=pl.BlockSpec(memory_space=pl.ANY),
        scratch_shapes=[
            pltpu.SMEM((1, 1, ts), jnp.int32),
            pltpu.SMEM((1, npad), jnp.int32),
            pltpu.VMEM((ZERO_ROWS, XROW), jnp.float32),
            pltpu.SemaphoreType.DMA(()),
            pltpu.SemaphoreType.DMA(()),
        ],
        compiler_params=_cparams(("arbitrary",)), name="moe_scatter",
    )(pos, padpos, xn)


def _expert_kernel(ea_ref, eb_ref, nu_ref, x_ref, w1a_ref, w3a_ref, w2a_ref, w1b_ref, w3b_ref, w2b_ref, y_ref):
    @pl.when(pl.program_id(0) < nu_ref[0])
    def _():
        x = _bf(x_ref[:, 0:D_MODEL])
        wrec = x_ref[:, D_MODEL:XROW]
        lane = _iota(wrec.shape, 1)
        wa = jnp.sum(jnp.where(lane == 0, wrec, 0.0), axis=-1, keepdims=True)
        wb = jnp.sum(jnp.where(lane == 1, wrec, 0.0), axis=-1, keepdims=True)

        def expert(w1_ref, w3_ref, w2_ref):
            a = _dot(x, w1_ref[...])
            hid = a * (1.0 / (1.0 + jnp.exp(-a))) * _dot(x, w3_ref[...])
            return _dot(_bf(hid), w2_ref[...])

        y_ref[...] = wa * expert(w1a_ref, w3a_ref, w2a_ref) + wb * expert(w1b_ref, w3b_ref, w2b_ref)

    @pl.when(pl.program_id(0) >= nu_ref[0])
    def _():
        y_ref[...] = jnp.zeros_like(y_ref)


def _experts(xs, tile_ea, tile_eb, n_used, lw, te):
    p_rows = xs.shape[0]

    def row(t, ea, eb, nu):
        return (t, 0)

    def wa(t, ea, eb, nu):
        return (ea[t], 0, 0)

    def wb(t, ea, eb, nu):
        return (eb[t], 0, 0)

    grid_spec = pltpu.PrefetchScalarGridSpec(
        num_scalar_prefetch=3,
        grid=(p_rows // te,),
        in_specs=[
            pl.BlockSpec((te, XROW), row),
            pl.BlockSpec((None, D_MODEL, D_EXPERT), wa),
            pl.BlockSpec((None, D_MODEL, D_EXPERT), wa),
            pl.BlockSpec((None, D_EXPERT, D_MODEL), wa),
            pl.BlockSpec((None, D_MODEL, D_EXPERT), wb),
            pl.BlockSpec((None, D_MODEL, D_EXPERT), wb),
            pl.BlockSpec((None, D_EXPERT, D_MODEL), wb),
        ],
        out_specs=pl.BlockSpec((te, D_MODEL), row),
    )
    return pl.pallas_call(
        _expert_kernel, grid_spec=grid_spec,
        out_shape=jax.ShapeDtypeStruct((p_rows, D_MODEL), jnp.float32),
        compiler_params=_cparams(("arbitrary",)), name="moe_experts",
    )(tile_ea, tile_eb, n_used, xs, lw["w1"], lw["w3"], lw["w2"], lw["w1"], lw["w3"], lw["w2"])


def _combine_ple_kernel(pos_ref, posn_ref, ys_ref, h1_ref, p_ref, gple_ref, wpg_ref, wpp_ref,
                        o_ref, pos_smem, buf_ref, sem_idx, sem_rows):
    i = pl.program_id(0)
    nsteps = pl.num_programs(0)
    tc = h1_ref.shape[0]
    slot = lax.rem(i, 2)

    def issue_tile(src_pos_ref, s):
        cp = pltpu.make_async_copy(src_pos_ref, pos_smem, sem_idx)
        cp.start()
        cp.wait()

        def issue(r2, carry):
            _row_copy(ys_ref, pos_smem[0, 0, 2 * r2], buf_ref.at[s], 2 * r2, sem_rows.at[s]).start(priority=0)
            _row_copy(ys_ref, pos_smem[0, 0, 2 * r2 + 1], buf_ref.at[s], 2 * r2 + 1,
                      sem_rows.at[s]).start(priority=1)
            return carry

        lax.fori_loop(0, tc // 2, issue, 0, unroll=8)

    @pl.when(i == 0)
    def _():
        issue_tile(pos_ref, 0)

    @pl.when(i + 1 < nsteps)
    def _():
        issue_tile(posn_ref, 1 - slot)

    def drain(r, carry):
        _row_copy(ys_ref, 0, buf_ref.at[slot], 0, sem_rows.at[slot]).wait()
        return carry

    lax.fori_loop(0, tc, drain, 0, unroll=8)

    h2 = h1_ref[...] + buf_ref[slot]
    xn = h2 * lax.rsqrt(jnp.mean(h2 * h2, axis=-1, keepdims=True) + EPS) * gple_ref[...]
    gate = 1.0 / (1.0 + jnp.exp(-_dot(_bf(xn), wpg_ref[...])))
    o_ref[...] = h2 + _dot(_bf(p_ref[...]), wpp_ref[...]) * gate


def _combine_ple(ys, pos, h1, p_l, lw, tc):
    n = h1.shape[0]
    nsteps = n // tc

    def row(i):
        return (i, 0)

    def full(i):
        return (0, 0)

    return pl.pallas_call(
        _combine_ple_kernel,
        out_shape=jax.ShapeDtypeStruct((n, D_MODEL), jnp.float32),
        grid=(nsteps,),
        in_specs=[
            pl.BlockSpec((1, 1, tc), lambda i: (i, 0, 0)),
            pl.BlockSpec((1, 1, tc), lambda i: (jnp.minimum(i + 1, nsteps - 1), 0, 0)),
            pl.BlockSpec(memory_space=pl.ANY),
            pl.BlockSpec((tc, D_MODEL), row),
            pl.BlockSpec((tc, PLE_DIM), row),
            pl.BlockSpec((1, D_MODEL), full),
            pl.BlockSpec((D_MODEL, D_MODEL), full),
            pl.BlockSpec((PLE_DIM, D_MODEL), full),
        ],
        out_specs=pl.BlockSpec((tc, D_MODEL), row),
        scratch_shapes=[
            pltpu.SMEM((1, 1, tc), jnp.int32),
            pltpu.VMEM((2, tc, D_MODEL), jnp.float32),
            pltpu.SemaphoreType.DMA(()),
            pltpu.SemaphoreType.DMA((2,)),
        ],
        compiler_params=_cparams(("arbitrary",)), name="moe_combine_ple",
    )(pos, pos, ys, h1, p_l, lw["g_ple"], lw["w_pg"], lw["w_pp"])


def _moe_ple(h1, xn, route, counts, p_l, lw, ts, tc, te):
    n = h1.shape[0]
    p_rows = n + N_CLASSES * te
    i32 = jnp.int32
    cnt = counts[0, 0:N_CLASSES].astype(i32)
    padded = ((cnt + te - 1) // te) * te
    pend = jnp.cumsum(padded)
    pstart = pend - padded
    cls = route[:, 0].astype(i32)
    rank = route[:, 1].astype(i32)
    classes = jnp.arange(N_CLASSES, dtype=i32)
    pos = jnp.sum(jnp.where(cls[:, None] == classes[None, :], pstart[None, :], 0), axis=1) + rank
    n_tiles = p_rows // te
    tile_row = jnp.arange(n_tiles, dtype=i32) * te
    tile_cls = jnp.minimum(jnp.sum((pend[None, :] <= tile_row[:, None]).astype(i32), axis=1), N_CLASSES - 1)
    pair = tile_cls % PAIRS_PER_GROUP
    group = tile_cls // PAIRS_PER_GROUP
    pair_hot = pair[:, None] == jnp.arange(PAIRS_PER_GROUP, dtype=i32)[None, :]
    tile_ea = group * EXPERTS_PER_GROUP + jnp.sum(jnp.where(pair_hot, jnp.asarray(PAIR_A, i32)[None, :], 0), axis=1)
    tile_eb = group * EXPERTS_PER_GROUP + jnp.sum(jnp.where(pair_hot, jnp.asarray(PAIR_B, i32)[None, :], 0), axis=1)
    n_used = (pend[-1] // te).astype(i32).reshape(1)
    assert te == 2 * ZERO_ROWS
    padpos = jnp.stack([pend, padded - cnt], axis=1).reshape(2 * N_CLASSES)
    tail = jnp.stack([pend[-1], (p_rows - pend[-1]) // ZERO_ROWS])
    padpos = jnp.pad(jnp.concatenate([padpos, tail]), (0, 128 - 2 * N_CLASSES - 2)).astype(i32)[None, :]

    xs = _scatter_rows(xn, pos.reshape(n // ts, 1, ts), padpos, p_rows, ts)
    ys = _experts(xs, tile_ea.astype(i32), tile_eb.astype(i32), n_used, lw, te)
    return _combine_ple(ys, pos.reshape(n // tc, 1, tc), h1, p_l, lw, tc)


_B_HEAD_ORDER = np.array([0, 4, 1, 5, 2, 6, 3, 7])


def _block_mean_matrix(width, block):
    idx = np.arange(width)
    return ((idx[:, None] // block) == (idx[None, :] // block)).astype(np.float32) / block


def _rotate_half_matrix(width, block):
    half = block // 2
    m = np.zeros((width, width), np.float32)
    for j in range(width):
        if j % block < half:
            m[j + half, j] = -1.0
        else:
            m[j - half, j] = 1.0
    return m


def _constants():
    bf16 = jnp.bfloat16
    return {
        "m32": jnp.asarray(_block_mean_matrix(256, A_DH), bf16),
        "m64": jnp.asarray(_block_mean_matrix(256, HEAD_DIM), bf16),
        "rot": jnp.asarray(_rotate_half_matrix(256, A_DH), bf16),
    }


def _rope_tables(seq):
    def table(pos, dim):
        inv = 1.0 / (ROPE_THETA ** (jnp.arange(0, dim, 2, dtype=jnp.float32) / dim))
        ang = pos.astype(jnp.float32)[:, None] * inv[None, :]
        return jnp.cos(ang), jnp.sin(ang)

    rows = seq // GRID_W
    pos = jnp.arange(seq)
    row = jnp.repeat(jnp.arange(rows), GRID_W)
    col = jnp.tile(jnp.arange(GRID_W), rows)
    c1, s1 = table(pos, A_DH)
    cr, sr = table(row, HEAD_DIM // 2)
    cc, sc = table(col, HEAD_DIM // 2)
    return {
        "cosa": jnp.tile(jnp.concatenate([c1, c1], axis=1), (1, 4)),
        "sina": jnp.tile(jnp.concatenate([s1, s1], axis=1), (1, 4)),
        "cosb": jnp.tile(jnp.concatenate([cr, cr, cc, cc], axis=1), (1, 2)),
        "sinb": jnp.tile(jnp.concatenate([sr, sr, sc, sc], axis=1), (1, 2)),
    }


def _prep_layer(l, g_mix, w_in, a_q_norm, a_k_norm, a_lambda, a_subln, b_q_norm, b_k_norm, c_gate_w,
                c_gate_b, c_out_norm, w_out, g_ffn, w_rg, w_re, w_e1, w_e3, w_e2, g_ple, w_pg, w_pp):
    f32 = jnp.float32
    bf16 = jnp.bfloat16
    aq, ak, av = w_in[:, 0:256], w_in[:, 256:512], w_in[:, 512:768]
    bq, bk, bv = w_in[:, 768:1280], w_in[:, 1280:1408], w_in[:, 1408:1536]
    crest = w_in[:, 1536:2336]
    bq = bq.reshape(D_MODEL, B_HEADS, HEAD_DIM)[:, _B_HEAD_ORDER, :].reshape(D_MODEL, 512)
    w_main = jnp.concatenate(
        [aq, ak, bq, bk, crest, jnp.zeros((D_MODEL, MAIN_COLS - QK_COLS - 800), f32)], axis=1)
    w_vt = jnp.concatenate([av, bv], axis=1).T

    gn = jnp.concatenate([
        jnp.tile(a_q_norm, 8) * (A_DH ** -0.5 * LOG2E),
        jnp.tile(a_k_norm, 8),
        jnp.tile(b_q_norm, 8) * (HEAD_DIM ** -0.5 * LOG2E),
        jnp.tile(b_k_norm, 2),
    ])[None, :]

    gw = jnp.zeros((128, 256), f32)
    gw = gw.at[0:C_GATE_RANK, 0:128].set(c_gate_w[0])
    gw = gw.at[C_GATE_RANK:2 * C_GATE_RANK, 128:256].set(c_gate_w[1])
    gb = jnp.concatenate([c_gate_b[0], c_gate_b[1]])[None, :]

    gq_a = jnp.max(jnp.abs(a_q_norm)) * (A_DH ** -0.5 * LOG2E)
    gq_b = jnp.max(jnp.abs(b_q_norm)) * (HEAD_DIM ** -0.5 * LOG2E)
    bound = 1.05 * jnp.maximum(A_DH * gq_a * jnp.max(jnp.abs(a_k_norm)),
                               HEAD_DIM * gq_b * jnp.max(jnp.abs(b_k_norm)))
    bounded = bound <= SCORE_BOUND

    lam_init = 0.8 - 0.6 * math.exp(-0.3 * l)
    lf = a_lambda.astype(f32)
    lam = jnp.exp(jnp.sum(lf[0] * lf[1])) - jnp.exp(jnp.sum(lf[2] * lf[3])) + lam_init

    wo_b = w_out[256:768].reshape(B_HEADS, HEAD_DIM, D_MODEL)[_B_HEAD_ORDER].reshape(512, D_MODEL)
    w_out_p = jnp.concatenate([w_out[0:256], wo_b, w_out[768:1024]], axis=0)

    wr = jnp.zeros((D_MODEL, ROUTE_LANES), f32)
    wr = wr.at[:, 0:N_GROUPS].set(w_rg)
    wr = wr.at[:, ROUTE_OFF:ROUTE_OFF + N_EXPERTS].set(
        jnp.transpose(w_re, (1, 0, 2)).reshape(D_MODEL, N_EXPERTS))
    wr_hi = wr.astype(bf16)
    wr_lo = (wr - wr_hi.astype(f32)).astype(bf16)

    return {
        "g_mix": g_mix[None, :], "w_main": w_main.astype(bf16), "w_vt": w_vt.astype(bf16), "gn": gn,
        "gw": gw.astype(bf16), "gb": gb, "lam": jnp.reshape(lam, (1, 1)), "bounded": bounded,
        "gsub": (a_subln * (1.0 - lam_init))[:, None],
        "gc": jnp.tile(c_out_norm, C_HEADS)[None, :], "w_out": w_out_p.astype(bf16),
        "g_ffn": g_ffn[None, :], "wr_t": jnp.concatenate([wr_hi.T, wr_lo.T], axis=0),
        "w1": w_e1.astype(bf16), "w3": w_e3.astype(bf16), "w2": w_e2.astype(bf16),
        "g_ple": g_ple[None, :], "w_pg": w_pg.astype(bf16), "w_pp": w_pp.astype(bf16),
    }


def _tile(limit, total):
    t = min(limit, total)
    assert total % t == 0
    return t


def _trunk(x, p, layers, consts):
    batch, seq, _ = x.shape
    n = batch * seq
    tabs = _rope_tables(seq)
    tm = _tile(512, seq)
    ts = _tile(2048, seq)
    tc = _tile(1024, seq)
    te = 512
    h = x.reshape(n, D_MODEL)
    for l in range(DEPTH):
        lw = layers[l]
        qa, ka, qb, kb, vt, cqk, la, cv, cgate = _in_proj(h, lw, tabs, consts, seq, tm)
        oa = lax.cond(lw["bounded"],
                      lambda *a: _diff_attn(*a, batch, seq, True),
                      lambda *a: _diff_attn(*a, batch, seq, False),
                      qa, ka, vt, lw["lam"], lw["gsub"])
        ob = lax.cond(lw["bounded"],
                      lambda *a: _gqa_attn(*a, batch, seq, True),
                      lambda *a: _gqa_attn(*a, batch, seq, False),
                      qb, kb, vt)
        gla_ok = GLA_CHUNK * jnp.max(-la) <= GLA_DECAY_BOUND
        o_f, o_b = lax.cond(
            gla_ok,
            lambda *a: (_gla(*a, batch, seq, False, True), _gla(*a, batch, seq, True, True)),
            lambda *a: (_gla(*a, batch, seq, False, False), _gla(*a, batch, seq, True, False)),
            cqk, la, cv)
        h1, xn, route, counts = _out_proj(h, oa, ob, o_f, o_b, cgate, lw, consts, tm)
        h = _moe_ple(h1, xn, route, counts, p[l].reshape(n, PLE_DIM), lw, ts, tc, te)
    return h.reshape(batch, seq, D_MODEL)


def kernel(x_prompt, x_sample, p_prompt, p_sample, g_mix, w_in, a_q_norm, a_k_norm, a_lambda, a_subln, b_q_norm, b_k_norm, c_gate_w, c_gate_b, c_out_norm, w_out, g_ffn, w_router_group, w_router_expert, w_exp_gate, w_exp_up, w_exp_down, g_ple, w_ple_gate, w_ple_proj):
    consts = _constants()
    layers = [
        _prep_layer(l, g_mix[l], w_in[l], a_q_norm[l], a_k_norm[l], a_lambda[l], a_subln[l], b_q_norm[l],
                    b_k_norm[l], c_gate_w[l], c_gate_b[l], c_out_norm[l], w_out[l], g_ffn[l],
                    w_router_group[l], w_router_expert[l], w_exp_gate[l], w_exp_up[l], w_exp_down[l],
                    g_ple[l], w_ple_gate[l], w_ple_proj[l])
        for l in range(DEPTH)
    ]
    y_prompt = _trunk(x_prompt, p_prompt, layers, consts)
    y_sample = _trunk(x_sample, p_sample, layers, consts)
    return (y_prompt, y_sample)
```

```python
import functools
import math

import jax
import jax.numpy as jnp
import numpy as np
from jax import lax
from jax.experimental import pallas as pl
from jax.experimental.pallas import tpu as pltpu

D_MODEL = 1024
DEPTH = 4
HEAD_DIM = 64
A_HEADS = 4
A_DH = 32
B_HEADS = 8
B_KV_HEADS = 2
C_HEADS = 4
C_DK = 32
C_DV = 64
C_GATE_RANK = 16
GATE_TAU = 16.0
GLA_CHUNK = 64
GLA_SUB = 16
GRID_W = 64
ROPE_THETA = 10000.0
PLE_DIM = 256
N_GROUPS = 4
EXPERTS_PER_GROUP = 4
N_EXPERTS = 16
D_EXPERT = 512
EPS = 1e-6
LOG2E = 1.4426950408889634

VMEM_LIMIT_BYTES = 52 * 1024 * 1024
NEG_INF = float("-inf")

QK_COLS = 1152
MAIN_COLS = 2048
VT_HEAD_ROWS = 80
V_ROWS = 384
VT_ROWS = 6 * VT_HEAD_ROWS
TQ_BOUNDED = 512
TQ_SHIFTED = 256
SCORE_BOUND = 60.0
GLA_DECAY_BOUND = 80.0
TB_BOUNDED = 2048
TB_EXACT = 256
ZERO_ROWS = 256
SUBLANES = 8
ROUTE_LANES = 128
ROUTE_OFF = 4
PAIRS_PER_GROUP = 6
N_CLASSES = N_GROUPS * PAIRS_PER_GROUP
PAIR_A = (0, 0, 0, 1, 1, 2)
PAIR_B = (1, 2, 3, 2, 3, 3)
XROW = D_MODEL + 128


def _cparams(sem):
    return pltpu.CompilerParams(dimension_semantics=sem, vmem_limit_bytes=VMEM_LIMIT_BYTES)


def _bf(x):
    return x.astype(jnp.bfloat16)


def _dot(a, b):
    return jnp.dot(a, b, preferred_element_type=jnp.float32)


def _dot_nt(a, b):
    return lax.dot_general(a, b, (((1,), (1,)), ((), ())), preferred_element_type=jnp.float32)


def _dot_tn(a, b):
    return lax.dot_general(a, b, (((0,), (0,)), ((), ())), preferred_element_type=jnp.float32)


def _iota(shape, dim):
    return lax.broadcasted_iota(jnp.int32, shape, dim)


def _in_proj_kernel(h_ref, gmix_ref, w_ref, wvt_ref, gn_ref, cosa_ref, sina_ref, cosb_ref, sinb_ref,
                    m32_ref, m64_ref, rot_ref, gw_ref, gb_ref,
                    qa_ref, ka_ref, qb_ref, kb_ref, vt_ref, cqk_ref, la_ref, cv_ref, cg_ref):
    x = h_ref[...]
    a = x * lax.rsqrt(jnp.mean(x * x, axis=-1, keepdims=True) + EPS) * gmix_ref[...]
    ab = _bf(a)
    proj = _dot(ab, w_ref[...])
    vt = _bf(_dot_nt(wvt_ref[...], ab))
    ones = jnp.ones((VT_HEAD_ROWS - HEAD_DIM, vt.shape[1]), jnp.bfloat16)
    for hh in range(VT_ROWS // VT_HEAD_ROWS):
        vt_ref[VT_HEAD_ROWS * hh:VT_HEAD_ROWS * hh + HEAD_DIM, :] = vt[HEAD_DIM * hh:HEAD_DIM * (hh + 1), :]
        vt_ref[VT_HEAD_ROWS * hh + HEAD_DIM:VT_HEAD_ROWS * (hh + 1), :] = ones

    cosa = cosa_ref[...]
    sina = sina_ref[...]
    cosb = cosb_ref[...]
    sinb = sinb_ref[...]
    cosa2 = jnp.concatenate([cosa, cosa], axis=1)
    sina2 = jnp.concatenate([sina, sina], axis=1)
    cosb2 = jnp.concatenate([cosb, cosb], axis=1)
    sinb2 = jnp.concatenate([sinb, sinb], axis=1)

    def qk_slab(off, width, mean_mat, cos, sin, out_ref, out_off):
        xs = proj[:, off:off + width]
        ms = _dot(_bf(xs * xs), mean_mat)
        xn = xs * lax.rsqrt(ms + EPS) * gn_ref[:, off:off + width]
        rot = _dot(_bf(xn), rot_ref[0:width, 0:width])
        out_ref[:, out_off:out_off + width] = _bf(xn * cos + rot * sin)

    m32 = m32_ref[...]
    m64 = m64_ref[...]
    qk_slab(0, 256, m32, cosa2, sina2, qa_ref, 0)
    qk_slab(256, 256, m32, cosa2, sina2, ka_ref, 0)
    qk_slab(512, 256, m64, cosb2, sinb2, qb_ref, 0)
    qk_slab(768, 256, m64, cosb2, sinb2, qb_ref, 256)
    qk_slab(1024, 128, m64_ref[0:128, 0:128], cosb, sinb, kb_ref, 0)

    c0 = QK_COLS
    cqk_ref[:, 0:128] = proj[:, c0:c0 + 128] * (C_DK ** -0.5)
    cqk_ref[:, 128:256] = proj[:, c0 + 128:c0 + 256]
    cv_ref[...] = proj[:, c0 + 256:c0 + 512]
    cg = proj[:, c0 + 512:c0 + 768]
    cg_ref[...] = cg * (1.0 / (1.0 + jnp.exp(-cg)))
    clr = proj[:, c0 + 768:c0 + 896]
    z = _dot(_bf(clr), gw_ref[...]) + gb_ref[...]
    la_ref[...] = (jnp.minimum(z, 0.0) - jnp.log(1.0 + jnp.exp(-jnp.abs(z)))) * (1.0 / GATE_TAU)


def _in_proj(h, lw, tabs, consts, seq, tm):
    n = h.shape[0]
    nt = n // tm
    spb = seq // tm

    def row(i):
        return (i, 0)

    def full(i):
        return (0, 0)

    def tab(i):
        return (i % spb, 0)

    f32 = jnp.float32
    bf16 = jnp.bfloat16
    out_shape = (
        jax.ShapeDtypeStruct((n, 256), bf16),
        jax.ShapeDtypeStruct((n, 256), bf16),
        jax.ShapeDtypeStruct((n, 512), bf16),
        jax.ShapeDtypeStruct((n, 128), bf16),
        jax.ShapeDtypeStruct((VT_ROWS, n), bf16),
        jax.ShapeDtypeStruct((n, 256), f32),
        jax.ShapeDtypeStruct((n, 256), f32),
        jax.ShapeDtypeStruct((n, 256), f32),
        jax.ShapeDtypeStruct((n, 256), f32),
    )
    in_specs = [
        pl.BlockSpec((tm, D_MODEL), row),
        pl.BlockSpec((1, D_MODEL), full),
        pl.BlockSpec((D_MODEL, MAIN_COLS), full),
        pl.BlockSpec((V_ROWS, D_MODEL), full),
        pl.BlockSpec((1, QK_COLS), full),
        pl.BlockSpec((tm, 128), tab),
        pl.BlockSpec((tm, 128), tab),
        pl.BlockSpec((tm, 128), tab),
        pl.BlockSpec((tm, 128), tab),
        pl.BlockSpec((256, 256), full),
        pl.BlockSpec((256, 256), full),
        pl.BlockSpec((256, 256), full),
        pl.BlockSpec((128, 256), full),
        pl.BlockSpec((1, 256), full),
    ]
    out_specs = (
        pl.BlockSpec((tm, 256), row),
        pl.BlockSpec((tm, 256), row),
        pl.BlockSpec((tm, 512), row),
        pl.BlockSpec((tm, 128), row),
        pl.BlockSpec((VT_ROWS, tm), lambda i: (0, i)),
        pl.BlockSpec((tm, 256), row),
        pl.BlockSpec((tm, 256), row),
        pl.BlockSpec((tm, 256), row),
        pl.BlockSpec((tm, 256), row),
    )
    return pl.pallas_call(
        _in_proj_kernel, out_shape=out_shape, grid=(nt,), in_specs=in_specs, out_specs=out_specs,
        compiler_params=_cparams(("parallel",)), name="in_proj",
    )(h, lw["g_mix"], lw["w_main"], lw["w_vt"], lw["gn"], tabs["cosa"], tabs["sina"], tabs["cosb"],
      tabs["sinb"], consts["m32"], consts["m64"], consts["rot"], lw["gw"], lw["gb"])


def _softmax_pv(k, qm, vt, bounded):
    st = _dot_nt(k, qm)
    if bounded:
        p = jnp.exp2(st)
    else:
        p = jnp.exp2(st - jnp.max(st, axis=0, keepdims=True))
    ot = _dot(vt, _bf(p))
    return ot[0:HEAD_DIM, :] / ot[HEAD_DIM:HEAD_DIM + 1, :]


def _diff_attn_kernel(q_ref, k_ref, vt_ref, lam_ref, gsub_ref, o_ref, *, bounded):
    q = q_ref[...]
    k = k_ref[...]
    lane = _iota(q.shape, 1)
    zero = jnp.zeros_like(q)
    lam = lam_ref[...]
    gsub = gsub_ref[...]
    outs = []
    for h in range(A_HEADS):
        vt = vt_ref[VT_HEAD_ROWS * h:VT_HEAD_ROWS * (h + 1), :]
        comp = []
        for c in range(2):
            lo = A_DH * (2 * h + c)
            qm = jnp.where((lane >= lo) & (lane < lo + A_DH), q, zero)
            comp.append(_softmax_pv(k, qm, vt, bounded))
        od = comp[0] - lam * comp[1]
        r = lax.rsqrt(jnp.mean(od * od, axis=0, keepdims=True) + EPS)
        outs.append(od * r * gsub)
    ot = jnp.concatenate(outs, axis=0)
    o_ref[...] = _bf(ot.T)


def _gqa_attn_kernel(q_ref, k_ref, vt_ref, o_ref, *, bounded):
    k = k_ref[...]
    outs = []
    for j in range(B_HEADS // B_KV_HEADS):
        q = q_ref[:, 128 * j:128 * (j + 1)]
        lane = _iota(q.shape, 1)
        zero = jnp.zeros_like(q)
        for g in range(B_KV_HEADS):
            qm = jnp.where((lane >= HEAD_DIM * g) & (lane < HEAD_DIM * (g + 1)), q, zero)
            vt = vt_ref[VT_HEAD_ROWS * g:VT_HEAD_ROWS * (g + 1), :]
            outs.append(_softmax_pv(k, qm, vt, bounded))
    ot = jnp.concatenate(outs, axis=0)
    o_ref[...] = _bf(ot.T)


def _diff_attn(qa, ka, vt, lam, gsub, batch, seq, bounded):
    n = qa.shape[0]
    tq = _tile(TQ_BOUNDED if bounded else TQ_SHIFTED, seq)
    nq = seq // tq
    return pl.pallas_call(
        functools.partial(_diff_attn_kernel, bounded=bounded),
        out_shape=jax.ShapeDtypeStruct((n, 256), jnp.bfloat16),
        grid=(batch, nq),
        in_specs=[
            pl.BlockSpec((tq, 256), lambda b, i: (b * nq + i, 0)),
            pl.BlockSpec((seq, 256), lambda b, i: (b, 0)),
            pl.BlockSpec((A_HEADS * VT_HEAD_ROWS, seq), lambda b, i: (0, b)),
            pl.BlockSpec((1, 1), lambda b, i: (0, 0)),
            pl.BlockSpec((HEAD_DIM, 1), lambda b, i: (0, 0)),
        ],
        out_specs=pl.BlockSpec((tq, 256), lambda b, i: (b * nq + i, 0)),
        compiler_params=_cparams(("parallel", "parallel")),
        name="diff_attn" if bounded else "diff_attn_shifted",
    )(qa, ka, vt, lam, gsub)


def _gqa_attn(qb, kb, vt, batch, seq, bounded):
    n = qb.shape[0]
    tq = _tile(TQ_BOUNDED if bounded else TQ_SHIFTED, seq)
    nq = seq // tq
    return pl.pallas_call(
        functools.partial(_gqa_attn_kernel, bounded=bounded),
        out_shape=jax.ShapeDtypeStruct((n, 512), jnp.bfloat16),
        grid=(batch, nq),
        in_specs=[
            pl.BlockSpec((tq, 512), lambda b, i: (b * nq + i, 0)),
            pl.BlockSpec((seq, 128), lambda b, i: (b, 0)),
            pl.BlockSpec((B_KV_HEADS * VT_HEAD_ROWS, seq), lambda b, i: (A_HEADS // B_KV_HEADS, b)),
        ],
        out_specs=pl.BlockSpec((tq, 512), lambda b, i: (b * nq + i, 0)),
        compiler_params=_cparams(("parallel", "parallel")),
        name="gqa_attn" if bounded else "gqa_attn_shifted",
    )(qb, kb, vt)


def _split3(x):
    hi = _bf(x)
    r1 = x - hi.astype(jnp.float32)
    mid = _bf(r1)
    lo = _bf(r1 - mid.astype(jnp.float32))
    return hi, mid, lo


def _gla_kernel(qk_ref, g_ref, v_ref, o_ref, st_ref, *, nchunk, rev):
    C = GLA_CHUNK
    SB = GLA_SUB
    nsub = C // SB

    @pl.when(pl.program_id(1) == 0)
    def _():
        st_ref[...] = jnp.zeros_like(st_ref)

    r64 = _iota((C, C), 0)
    c64 = _iota((C, C), 1)
    tri = _bf(jnp.where((c64 >= r64) if rev else (c64 <= r64), 1.0, 0.0))
    hm_k = (_iota((4 * C, 128), 0) >> 6) == (_iota((4 * C, 128), 1) >> 5)
    hm_v = (_iota((4 * C, 256), 0) >> 6) == (_iota((4 * C, 256), 1) >> 6)
    ones_exp = _bf(jnp.where((_iota((128, 256), 0) >> 5) == (_iota((128, 256), 1) >> 6), 1.0, 0.0))
    rowc = _iota((C, 128), 0)
    rows = _iota((SB, 128), 0)

    order = range(nchunk - 1, -1, -1) if rev else range(nchunk)
    for ci in order:
        r0 = ci * C
        q = qk_ref[r0:r0 + C, 0:128]
        k = qk_ref[r0:r0 + C, 128:256]
        g = g_ref[r0:r0 + C, :]
        v = v_ref[r0:r0 + C, :]
        g1, g2, g3 = _split3(g)
        b = _dot(tri, g1) + _dot(tri, g2) + _dot(tri, g3)

        st = st_ref[...]
        o = _dot_nt(_bf(q * jnp.exp(b)), _bf(st))
        bl = b[0:1, :] if rev else b[C - 1:C, :]
        ke = k * jnp.exp(bl - b)
        dst = _dot_tn(_bf(v), _bf(ke))
        st_ref[...] = st * jnp.exp(bl) + jnp.where(hm_k, dst, 0.0)

        att = jnp.zeros((C, 4 * C), jnp.float32)
        for blk in (range(0, nsub - 1) if rev else range(1, nsub)):
            if rev:
                ref = b[SB * blk + SB - 1:SB * blk + SB, :]
                kmask = rowc >= SB * (blk + 1)
            else:
                ref = b[SB * blk:SB * blk + 1, :]
                kmask = rowc < SB * blk
            qmask = (rowc >= SB * blk) & (rowc < SB * (blk + 1))
            qd = q * jnp.exp(jnp.where(qmask, b - ref, NEG_INF))
            kd = k * jnp.exp(jnp.where(kmask, ref - b, NEG_INF))
            kexp = jnp.where(hm_k, jnp.concatenate([kd] * 4, axis=0), 0.0)
            att = att + _dot_nt(_bf(qd), _bf(kexp))
        vexp = jnp.where(hm_v, jnp.concatenate([v] * 4, axis=0), 0.0)
        o = o + _dot(_bf(att), _bf(vexp))

        diag = []
        for blk in range(nsub):
            s0 = SB * blk
            bs = b[s0:s0 + SB, :]
            qs = q[s0:s0 + SB, :]
            ks = k[s0:s0 + SB, :]
            vs = v[s0:s0 + SB, :]
            ws = []
            for j in range(SB):
                keep = (rows <= j) if rev else (rows >= j)
                e = jnp.exp(jnp.where(keep, bs - bs[j:j + 1, :], NEG_INF))
                ws.append(e * qs * ks[j:j + 1, :])
            p = _dot(_bf(jnp.concatenate(ws, axis=0)), ones_exp)
            od = p[0:SB, :] * vs[0:1, :]
            for j in range(1, SB):
                od = od + p[SB * j:SB * (j + 1), :] * vs[j:j + 1, :]
            diag.append(od)
        o_ref[r0:r0 + C, :] = o + jnp.concatenate(diag, axis=0)


def _gla_bounded_kernel(qk_ref, g_ref, v_ref, o_ref, st_ref, *, nchunk, rev):
    C = GLA_CHUNK

    @pl.when(pl.program_id(1) == 0)
    def _():
        st_ref[...] = jnp.zeros_like(st_ref)

    r64 = _iota((C, C), 0)
    c64 = _iota((C, C), 1)
    tri = _bf(jnp.where((c64 >= r64) if rev else (c64 <= r64), 1.0, 0.0))
    hm_k = (_iota((4 * C, 128), 0) >> 6) == (_iota((4 * C, 128), 1) >> 5)
    hm_v = (_iota((4 * C, 256), 0) >> 6) == (_iota((4 * C, 256), 1) >> 6)
    qpos = _iota((C, 4 * C), 0)
    kpos = _iota((C, 4 * C), 1) & (C - 1)
    keep = (kpos >= qpos) if rev else (kpos <= qpos)

    order = range(nchunk - 1, -1, -1) if rev else range(nchunk)
    rng = [(ci * C, (ci + 1) * C) for ci in order]
    bs = []
    for lo, hi in rng:
        g1, g2, g3 = _split3(g_ref[lo:hi, :])
        bs.append(_dot(tri, g1) + _dot(tri, g2) + _dot(tri, g3))
    qes = [_bf(qk_ref[lo:hi, 0:128] * jnp.exp(b)) for (lo, hi), b in zip(rng, bs)]
    kds = [qk_ref[lo:hi, 128:256] * jnp.exp(-b) for (lo, hi), b in zip(rng, bs)]
    ebls = [jnp.exp(b[0:1, :] if rev else b[C - 1:C, :]) for b in bs]
    vbs = [_bf(v_ref[lo:hi, :]) for lo, hi in rng]
    atts = [jnp.where(keep, _dot_nt(qe, _bf(jnp.where(hm_k, jnp.concatenate([kd] * 4, axis=0), 0.0))), 0.0)
            for qe, kd in zip(qes, kds)]
    zero = jnp.zeros((), jnp.bfloat16)
    intras = [_dot(_bf(att), jnp.where(hm_v, jnp.concatenate([vb] * 4, axis=0), zero))
              for att, vb in zip(atts, vbs)]
    dsts = [jnp.where(hm_k, _dot_tn(vb, _bf(kd * ebl)), 0.0) for vb, kd, ebl in zip(vbs, kds, ebls)]

    st = st_ref[...]
    for (lo, hi), qe, o_intra, ebl, dst in zip(rng, qes, intras, ebls, dsts):
        o_ref[lo:hi, :] = o_intra + _dot_nt(qe, _bf(st))
        st = st * ebl + dst
    st_ref[...] = st


def _gla(cqk, la, cv, batch, seq, rev, bounded):
    n = cqk.shape[0]
    tb = _tile(TB_BOUNDED if bounded else TB_EXACT, seq)
    nb = seq // tb

    def blk(b, i):
        return (b * nb + (nb - 1 - i if rev else i), 0)

    def gblk(b, i):
        return (b * nb + (nb - 1 - i if rev else i), 1 if rev else 0)

    return pl.pallas_call(
        functools.partial(_gla_bounded_kernel if bounded else _gla_kernel, nchunk=tb // GLA_CHUNK, rev=rev),
        out_shape=jax.ShapeDtypeStruct((n, 256), jnp.float32),
        grid=(batch, nb),
        in_specs=[
            pl.BlockSpec((tb, 256), blk),
            pl.BlockSpec((tb, 128), gblk),
            pl.BlockSpec((tb, 256), blk),
        ],
        out_specs=pl.BlockSpec((tb, 256), blk),
        scratch_shapes=[pltpu.VMEM((256, 128), jnp.float32)],
        compiler_params=_cparams(("parallel", "arbitrary")),
        name=("gla_bwd" if rev else "gla_fwd") + ("" if bounded else "_exact"),
    )(cqk, la, cv)


def _out_proj_kernel(h_ref, oa_ref, ob_ref, of_ref, obw_ref, cg_ref, gc_ref, m64_ref, wo_ref, gffn_ref,
                     wrt_ref, h1_ref, xn_ref, route_ref, cnt_ref):
    @pl.when(pl.program_id(0) == 0)
    def _():
        cnt_ref[...] = jnp.zeros_like(cnt_ref)

    oc = of_ref[...] + obw_ref[...]
    ms = _dot(_bf(oc * oc), m64_ref[...])
    oc = oc * lax.rsqrt(ms + EPS) * gc_ref[...] * cg_ref[...]
    h1 = (h_ref[...] + _dot(oa_ref[...], wo_ref[0:256, :]) + _dot(ob_ref[...], wo_ref[256:768, :])
          + _dot(_bf(oc), wo_ref[768:1024, :]))
    h1_ref[...] = h1
    xn = h1 * lax.rsqrt(jnp.mean(h1 * h1, axis=-1, keepdims=True) + EPS) * gffn_ref[...]
    xn_ref[:, 0:D_MODEL] = xn
    xh = _bf(xn)
    xl = _bf(xn - xh.astype(jnp.float32))
    lgt = _dot_nt(wrt_ref[...], xh)
    lgt = lgt[0:ROUTE_LANES, :] + lgt[ROUTE_LANES:2 * ROUTE_LANES, :] + _dot_nt(wrt_ref[0:ROUTE_LANES, :], xl)
    lg = lgt.T

    lane = _iota(lg.shape, 1)
    big = jnp.int32(1 << 20)
    is_g = lane < N_GROUPS
    mg = jnp.max(jnp.where(is_g, lg, NEG_INF), axis=-1, keepdims=True)
    sg = jnp.sum(jnp.where(is_g, jnp.exp(lg - mg), 0.0), axis=-1, keepdims=True)
    p_top = 1.0 / sg
    g_top = jnp.min(jnp.where(is_g & (lg == mg), lane, big), axis=-1, keepdims=True)
    e_lo = ROUTE_OFF + EXPERTS_PER_GROUP * g_top
    sel = (lane >= e_lo) & (lane < e_lo + EXPERTS_PER_GROUP)
    ev0 = jnp.max(jnp.where(sel, lg, NEG_INF), axis=-1, keepdims=True)
    i0 = jnp.min(jnp.where(sel & (lg == ev0), lane, big), axis=-1, keepdims=True)
    sel1 = sel & (lane != i0)
    ev1 = jnp.max(jnp.where(sel1, lg, NEG_INF), axis=-1, keepdims=True)
    i1 = jnp.min(jnp.where(sel1 & (lg == ev1), lane, big), axis=-1, keepdims=True)
    t = jnp.exp(ev1 - ev0)
    pe0 = p_top / (1.0 + t)
    pe1 = p_top * t / (1.0 + t)

    l0 = i0 - ROUTE_OFF - EXPERTS_PER_GROUP * g_top
    l1 = i1 - ROUTE_OFF - EXPERTS_PER_GROUP * g_top
    la = jnp.minimum(l0, l1)
    lb = jnp.maximum(l0, l1)
    cls = PAIRS_PER_GROUP * g_top + ((la * (7 - la)) >> 1) + (lb - la - 1)
    wa = jnp.where(l0 < l1, pe0, pe1)
    wb = jnp.where(l0 < l1, pe1, pe0)

    tm = lg.shape[0]
    picked = _bf(jnp.where(lane == cls, 1.0, 0.0))
    strict_lower = _bf(jnp.where(_iota((tm, tm), 1) < _iota((tm, tm), 0), 1.0, 0.0))
    prefix = _dot(strict_lower, picked) + cnt_ref[...]
    rank = jnp.sum(jnp.where(lane == cls, prefix, 0.0), axis=-1, keepdims=True)
    cnt_ref[...] += jnp.sum(picked.astype(jnp.float32), axis=0, keepdims=True)

    wrec = jnp.where(lane == 0, wa, jnp.where(lane == 1, wb, 0.0))
    xn_ref[:, D_MODEL:XROW] = wrec
    route_ref[...] = jnp.where(lane == 0, cls.astype(jnp.float32), jnp.where(lane == 1, rank, 0.0))


def _out_proj(h, oa, ob, o_f, o_b, cgate, lw, consts, tm):
    n = h.shape[0]

    def row(i):
        return (i, 0)

    def full(i):
        return (0, 0)

    return pl.pallas_call(
        _out_proj_kernel,
        out_shape=(jax.ShapeDtypeStruct((n, D_MODEL), jnp.float32),
                   jax.ShapeDtypeStruct((n, XROW), jnp.float32),
                   jax.ShapeDtypeStruct((n, ROUTE_LANES), jnp.float32),
                   jax.ShapeDtypeStruct((1, ROUTE_LANES), jnp.float32)),
        grid=(n // tm,),
        in_specs=[
            pl.BlockSpec((tm, D_MODEL), row),
            pl.BlockSpec((tm, 256), row),
            pl.BlockSpec((tm, 512), row),
            pl.BlockSpec((tm, 256), row),
            pl.BlockSpec((tm, 256), row),
            pl.BlockSpec((tm, 256), row),
            pl.BlockSpec((1, 256), full),
            pl.BlockSpec((256, 256), full),
            pl.BlockSpec((D_MODEL, D_MODEL), full),
            pl.BlockSpec((1, D_MODEL), full),
            pl.BlockSpec((2 * ROUTE_LANES, D_MODEL), full),
        ],
        out_specs=(pl.BlockSpec((tm, D_MODEL), row), pl.BlockSpec((tm, XROW), row),
                   pl.BlockSpec((tm, ROUTE_LANES), row), pl.BlockSpec((1, ROUTE_LANES), full)),
        compiler_params=_cparams(("arbitrary",)), name="out_proj",
    )(h, oa, ob, o_f, o_b, cgate, lw["gc"], consts["m64"], lw["w_out"], lw["g_ffn"], lw["wr_t"])


def _row_copy(src, src_row, dst, dst_row, sem):
    return pltpu.make_async_copy(src.at[pl.ds(src_row, 1), :], dst.at[pl.ds(dst_row, 1), :], sem)


def _scatter_kernel(pos_ref, pad_ref, x_ref, xs_ref, pos_smem, pad_smem, zrow_ref, sem_idx, sem_rows):
    ts = x_ref.shape[0]
    cp = pltpu.make_async_copy(pos_ref, pos_smem, sem_idx)
    cp.start()
    cp.wait()

    def issue(r2, carry):
        _row_copy(x_ref, 2 * r2, xs_ref, pos_smem[0, 0, 2 * r2], sem_rows).start(priority=0)
        _row_copy(x_ref, 2 * r2 + 1, xs_ref, pos_smem[0, 0, 2 * r2 + 1], sem_rows).start(priority=1)
        return carry

    lax.fori_loop(0, ts // 2, issue, 0, unroll=8)

    def drain(r, carry):
        _row_copy(x_ref, 0, xs_ref, 0, sem_rows).wait()
        return carry

    lax.fori_loop(0, ts, drain, 0, unroll=8)

    @pl.when(pl.program_id(0) == pl.num_programs(0) - 1)
    def _():
        zrow_ref[...] = jnp.zeros_like(zrow_ref)
        cz = pltpu.make_async_copy(pad_ref, pad_smem, sem_idx)
        cz.start()
        cz.wait()

        def fill(start_wait):
            for c in range(N_CLASSES):
                end = pad_smem[0, 2 * c]
                gap = pad_smem[0, 2 * c + 1]
                for j in range(SUBLANES - 1):
                    @pl.when(j < (gap & (SUBLANES - 1)))
                    def _(j=j, end=end, gap=gap):
                        cp = _row_copy(zrow_ref, 0, xs_ref, end - gap + j, sem_rows)
                        if start_wait:
                            cp.start()
                        else:
                            cp.wait()
                filled = 0
                run = ZERO_ROWS
                while run >= SUBLANES:
                    @pl.when((gap & run) != 0)
                    def _(end=end, filled=filled, run=run):
                        start = pl.multiple_of(end - filled - run, run)
                        cp = pltpu.make_async_copy(zrow_ref.at[pl.ds(0, run), :],
                                                   xs_ref.at[pl.ds(start, run), :], sem_rows)
                        if start_wait:
                            cp.start()
                        else:
                            cp.wait()
                    filled = filled + (gap & run)
                    run //= 2
            tail = pad_smem[0, 2 * N_CLASSES]
            ntail = pad_smem[0, 2 * N_CLASSES + 1]
            for j in range(2 * N_CLASSES):
                @pl.when(j < ntail)
                def _(j=j):
                    start = pl.multiple_of(tail + ZERO_ROWS * j, ZERO_ROWS)
                    cp = pltpu.make_async_copy(zrow_ref, xs_ref.at[pl.ds(start, ZERO_ROWS), :], sem_rows)
                    if start_wait:
                        cp.start()
                    else:
                        cp.wait()

        fill(True)
        fill(False)


def _scatter_rows(xn, pos, padpos, p_rows, ts):
    n = xn.shape[0]
    npad = padpos.shape[1]
    return pl.pallas_call(
        _scatter_kernel,
        out_shape=jax.ShapeDtypeStruct((p_rows, XROW), jnp.float32),
        grid=(n // ts,),
        in_specs=[
            pl.BlockSpec((1, 1, ts), lambda i: (i, 0, 0)),
            pl.BlockSpec((1, npad), lambda i: (0, 0)),
            pl.BlockSpec((ts, XROW), lambda i: (i, 0)),
        ],
        out_specs=pl.BlockSpec(memory_space=pl.ANY),
        scratch_shapes=[
            pltpu.SMEM((1, 1, ts), jnp.int32),
            pltpu.SMEM((1, npad), jnp.int32),
            pltpu.VMEM((ZERO_ROWS, XROW), jnp.float32),
            pltpu.SemaphoreType.DMA(()),
            pltpu.SemaphoreType.DMA(()),
        ],
        compiler_params=_cparams(("arbitrary",)), name="moe_scatter",
    )(pos, padpos, xn)


def _expert_kernel(ea_ref, eb_ref, nu_ref, x_ref, w1a_ref, w3a_ref, w2a_ref, w1b_ref, w3b_ref, w2b_ref, y_ref):
    @pl.when(pl.program_id(0) < nu_ref[0])
    def _():
        x = _bf(x_ref[:, 0:D_MODEL])
        wrec = x_ref[:, D_MODEL:XROW]
        lane = _iota(wrec.shape, 1)
        wa = jnp.sum(jnp.where(lane == 0, wrec, 0.0), axis=-1, keepdims=True)
        wb = jnp.sum(jnp.where(lane == 1, wrec, 0.0), axis=-1, keepdims=True)

        def expert(w1_ref, w3_ref, w2_ref):
            a = _dot(x, w1_ref[...])
            hid = a * (1.0 / (1.0 + jnp.exp(-a))) * _dot(x, w3_ref[...])
            return _dot(_bf(hid), w2_ref[...])

        y_ref[...] = wa * expert(w1a_ref, w3a_ref, w2a_ref) + wb * expert(w1b_ref, w3b_ref, w2b_ref)

    @pl.when(pl.program_id(0) >= nu_ref[0])
    def _():
        y_ref[...] = jnp.zeros_like(y_ref)


def _experts(xs, tile_ea, tile_eb, n_used, lw, te):
    p_rows = xs.shape[0]

    def row(t, ea, eb, nu):
        return (t, 0)

    def wa(t, ea, eb, nu):
        return (ea[t], 0, 0)

    def wb(t, ea, eb, nu):
        return (eb[t], 0, 0)

    grid_spec = pltpu.PrefetchScalarGridSpec(
        num_scalar_prefetch=3,
        grid=(p_rows // te,),
        in_specs=[
            pl.BlockSpec((te, XROW), row),
            pl.BlockSpec((None, D_MODEL, D_EXPERT), wa),
            pl.BlockSpec((None, D_MODEL, D_EXPERT), wa),
            pl.BlockSpec((None, D_EXPERT, D_MODEL), wa),
            pl.BlockSpec((None, D_MODEL, D_EXPERT), wb),
            pl.BlockSpec((None, D_MODEL, D_EXPERT), wb),
            pl.BlockSpec((None, D_EXPERT, D_MODEL), wb),
        ],
        out_specs=pl.BlockSpec((te, D_MODEL), row),
    )
    return pl.pallas_call(
        _expert_kernel, grid_spec=grid_spec,
        out_shape=jax.ShapeDtypeStruct((p_rows, D_MODEL), jnp.float32),
        compiler_params=_cparams(("arbitrary",)), name="moe_experts",
    )(tile_ea, tile_eb, n_used, xs, lw["w1"], lw["w3"], lw["w2"], lw["w1"], lw["w3"], lw["w2"])


def _combine_ple_kernel(pos_ref, posn_ref, ys_ref, h1_ref, p_ref, gple_ref, wpg_ref, wpp_ref,
                        o_ref, pos_smem, buf_ref, sem_idx, sem_rows):
    i = pl.program_id(0)
    nsteps = pl.num_programs(0)
    tc = h1_ref.shape[0]
    slot = lax.rem(i, 2)

    def issue_tile(src_pos_ref, s):
        cp = pltpu.make_async_copy(src_pos_ref, pos_smem, sem_idx)
        cp.start()
        cp.wait()

        def issue(r2, carry):
            _row_copy(ys_ref, pos_smem[0, 0, 2 * r2], buf_ref.at[s], 2 * r2, sem_rows.at[s]).start(priority=0)
            _row_copy(ys_ref, pos_smem[0, 0, 2 * r2 + 1], buf_ref.at[s], 2 * r2 + 1,
                      sem_rows.at[s]).start(priority=1)
            return carry

        lax.fori_loop(0, tc // 2, issue, 0, unroll=8)

    @pl.when(i == 0)
    def _():
        issue_tile(pos_ref, 0)

    @pl.when(i + 1 < nsteps)
    def _():
        issue_tile(posn_ref, 1 - slot)

    def drain(r, carry):
        _row_copy(ys_ref, 0, buf_ref.at[slot], 0, sem_rows.at[slot]).wait()
        return carry

    lax.fori_loop(0, tc, drain, 0, unroll=8)

    h2 = h1_ref[...] + buf_ref[slot]
    xn = h2 * lax.rsqrt(jnp.mean(h2 * h2, axis=-1, keepdims=True) + EPS) * gple_ref[...]
    gate = 1.0 / (1.0 + jnp.exp(-_dot(_bf(xn), wpg_ref[...])))
    o_ref[...] = h2 + _dot(_bf(p_ref[...]), wpp_ref[...]) * gate


def _combine_ple(ys, pos, h1, p_l, lw, tc):
    n = h1.shape[0]
    nsteps = n // tc

    def row(i):
        return (i, 0)

    def full(i):
        return (0, 0)

    return pl.pallas_call(
        _combine_ple_kernel,
        out_shape=jax.ShapeDtypeStruct((n, D_MODEL), jnp.float32),
        grid=(nsteps,),
        in_specs=[
            pl.BlockSpec((1, 1, tc), lambda i: (i, 0, 0)),
            pl.BlockSpec((1, 1, tc), lambda i: (jnp.minimum(i + 1, nsteps - 1), 0, 0)),
            pl.BlockSpec(memory_space=pl.ANY),
            pl.BlockSpec((tc, D_MODEL), row),
            pl.BlockSpec((tc, PLE_DIM), row),
            pl.BlockSpec((1, D_MODEL), full),
            pl.BlockSpec((D_MODEL, D_MODEL), full),
            pl.BlockSpec((PLE_DIM, D_MODEL), full),
        ],
        out_specs=pl.BlockSpec((tc, D_MODEL), row),
        scratch_shapes=[
            pltpu.SMEM((1, 1, tc), jnp.int32),
            pltpu.VMEM((2, tc, D_MODEL), jnp.float32),
            pltpu.SemaphoreType.DMA(()),
            pltpu.SemaphoreType.DMA((2,)),
        ],
        compiler_params=_cparams(("arbitrary",)), name="moe_combine_ple",
    )(pos, pos, ys, h1, p_l, lw["g_ple"], lw["w_pg"], lw["w_pp"])


def _moe_ple(h1, xn, route, counts, p_l, lw, ts, tc, te):
    n = h1.shape[0]
    p_rows = n + N_CLASSES * te
    i32 = jnp.int32
    cnt = counts[0, 0:N_CLASSES].astype(i32)
    padded = ((cnt + te - 1) // te) * te
    pend = jnp.cumsum(padded)
    pstart = pend - padded
    cls = route[:, 0].astype(i32)
    rank = route[:, 1].astype(i32)
    classes = jnp.arange(N_CLASSES, dtype=i32)
    pos = jnp.sum(jnp.where(cls[:, None] == classes[None, :], pstart[None, :], 0), axis=1) + rank
    n_tiles = p_rows // te
    tile_row = jnp.arange(n_tiles, dtype=i32) * te
    tile_cls = jnp.minimum(jnp.sum((pend[None, :] <= tile_row[:, None]).astype(i32), axis=1), N_CLASSES - 1)
    pair = tile_cls % PAIRS_PER_GROUP
    group = tile_cls // PAIRS_PER_GROUP
    pair_hot = pair[:, None] == jnp.arange(PAIRS_PER_GROUP, dtype=i32)[None, :]
    tile_ea = group * EXPERTS_PER_GROUP + jnp.sum(jnp.where(pair_hot, jnp.asarray(PAIR_A, i32)[None, :], 0), axis=1)
    tile_eb = group * EXPERTS_PER_GROUP + jnp.sum(jnp.where(pair_hot, jnp.asarray(PAIR_B, i32)[None, :], 0), axis=1)
    n_used = (pend[-1] // te).astype(i32).reshape(1)
    assert te == 2 * ZERO_ROWS
    padpos = jnp.stack([pend, padded - cnt], axis=1).reshape(2 * N_CLASSES)
    tail = jnp.stack([pend[-1], (p_rows - pend[-1]) // ZERO_ROWS])
    padpos = jnp.pad(jnp.concatenate([padpos, tail]), (0, 128 - 2 * N_CLASSES - 2)).astype(i32)[None, :]

    xs = _scatter_rows(xn, pos.reshape(n // ts, 1, ts), padpos, p_rows, ts)
    ys = _experts(xs, tile_ea.astype(i32), tile_eb.astype(i32), n_used, lw, te)
    return _combine_ple(ys, pos.reshape(n // tc, 1, tc), h1, p_l, lw, tc)


_B_HEAD_ORDER = np.array([0, 4, 1, 5, 2, 6, 3, 7])


def _block_mean_matrix(width, block):
    idx = np.arange(width)
    return ((idx[:, None] // block) == (idx[None, :] // block)).astype(np.float32) / block


def _rotate_half_matrix(width, block):
    half = block // 2
    m = np.zeros((width, width), np.float32)
    for j in range(width):
        if j % block < half:
            m[j + half, j] = -1.0
        else:
            m[j - half, j] = 1.0
    return m


def _constants():
    bf16 = jnp.bfloat16
    return {
        "m32": jnp.asarray(_block_mean_matrix(256, A_DH), bf16),
        "m64": jnp.asarray(_block_mean_matrix(256, HEAD_DIM), bf16),
        "rot": jnp.asarray(_rotate_half_matrix(256, A_DH), bf16),
    }


def _rope_tables(seq):
    def table(pos, dim):
        inv = 1.0 / (ROPE_THETA ** (jnp.arange(0, dim, 2, dtype=jnp.float32) / dim))
        ang = pos.astype(jnp.float32)[:, None] * inv[None, :]
        return jnp.cos(ang), jnp.sin(ang)

    rows = seq // GRID_W
    pos = jnp.arange(seq)
    row = jnp.repeat(jnp.arange(rows), GRID_W)
    col = jnp.tile(jnp.arange(GRID_W), rows)
    c1, s1 = table(pos, A_DH)
    cr, sr = table(row, HEAD_DIM // 2)
    cc, sc = table(col, HEAD_DIM // 2)
    return {
        "cosa": jnp.tile(jnp.concatenate([c1, c1], axis=1), (1, 4)),
        "sina": jnp.tile(jnp.concatenate([s1, s1], axis=1), (1, 4)),
        "cosb": jnp.tile(jnp.concatenate([cr, cr, cc, cc], axis=1), (1, 2)),
        "sinb": jnp.tile(jnp.concatenate([sr, sr, sc, sc], axis=1), (1, 2)),
    }


def _prep_layer(l, g_mix, w_in, a_q_norm, a_k_norm, a_lambda, a_subln, b_q_norm, b_k_norm, c_gate_w,
                c_gate_b, c_out_norm, w_out, g_ffn, w_rg, w_re, w_e1, w_e3, w_e2, g_ple, w_pg, w_pp):
    f32 = jnp.float32
    bf16 = jnp.bfloat16
    aq, ak, av = w_in[:, 0:256], w_in[:, 256:512], w_in[:, 512:768]
    bq, bk, bv = w_in[:, 768:1280], w_in[:, 1280:1408], w_in[:, 1408:1536]
    crest = w_in[:, 1536:2336]
    bq = bq.reshape(D_MODEL, B_HEADS, HEAD_DIM)[:, _B_HEAD_ORDER, :].reshape(D_MODEL, 512)
    w_main = jnp.concatenate(
        [aq, ak, bq, bk, crest, jnp.zeros((D_MODEL, MAIN_COLS - QK_COLS - 800), f32)], axis=1)
    w_vt = jnp.concatenate([av, bv], axis=1).T

    gn = jnp.concatenate([
        jnp.tile(a_q_norm, 8) * (A_DH ** -0.5 * LOG2E),
        jnp.tile(a_k_norm, 8),
        jnp.tile(b_q_norm, 8) * (HEAD_DIM ** -0.5 * LOG2E),
        jnp.tile(b_k_norm, 2),
    ])[None, :]

    gw = jnp.zeros((128, 256), f32)
    gw = gw.at[0:C_GATE_RANK, 0:128].set(c_gate_w[0])
    gw = gw.at[C_GATE_RANK:2 * C_GATE_RANK, 128:256].set(c_gate_w[1])
    gb = jnp.concatenate([c_gate_b[0], c_gate_b[1]])[None, :]

    gq_a = jnp.max(jnp.abs(a_q_norm)) * (A_DH ** -0.5 * LOG2E)
    gq_b = jnp.max(jnp.abs(b_q_norm)) * (HEAD_DIM ** -0.5 * LOG2E)
    bound = 1.05 * jnp.maximum(A_DH * gq_a * jnp.max(jnp.abs(a_k_norm)),
                               HEAD_DIM * gq_b * jnp.max(jnp.abs(b_k_norm)))
    bounded = bound <= SCORE_BOUND

    lam_init = 0.8 - 0.6 * math.exp(-0.3 * l)
    lf = a_lambda.astype(f32)
    lam = jnp.exp(jnp.sum(lf[0] * lf[1])) - jnp.exp(jnp.sum(lf[2] * lf[3])) + lam_init

    wo_b = w_out[256:768].reshape(B_HEADS, HEAD_DIM, D_MODEL)[_B_HEAD_ORDER].reshape(512, D_MODEL)
    w_out_p = jnp.concatenate([w_out[0:256], wo_b, w_out[768:1024]], axis=0)

    wr = jnp.zeros((D_MODEL, ROUTE_LANES), f32)
    wr = wr.at[:, 0:N_GROUPS].set(w_rg)
    wr = wr.at[:, ROUTE_OFF:ROUTE_OFF + N_EXPERTS].set(
        jnp.transpose(w_re, (1, 0, 2)).reshape(D_MODEL, N_EXPERTS))
    wr_hi = wr.astype(bf16)
    wr_lo = (wr - wr_hi.astype(f32)).astype(bf16)

    return {
        "g_mix": g_mix[None, :], "w_main": w_main.astype(bf16), "w_vt": w_vt.astype(bf16), "gn": gn,
        "gw": gw.astype(bf16), "gb": gb, "lam": jnp.reshape(lam, (1, 1)), "bounded": bounded,
        "gsub": (a_subln * (1.0 - lam_init))[:, None],
        "gc": jnp.tile(c_out_norm, C_HEADS)[None, :], "w_out": w_out_p.astype(bf16),
        "g_ffn": g_ffn[None, :], "wr_t": jnp.concatenate([wr_hi.T, wr_lo.T], axis=0),
        "w1": w_e1.astype(bf16), "w3": w_e3.astype(bf16), "w2": w_e2.astype(bf16),
        "g_ple": g_ple[None, :], "w_pg": w_pg.astype(bf16), "w_pp": w_pp.astype(bf16),
    }


def _tile(limit, total):
    t = min(limit, total)
    assert total % t == 0
    return t


def _trunk(x, p, layers, consts):
    batch, seq, _ = x.shape
    n = batch * seq
    tabs = _rope_tables(seq)
    tm = _tile(1024, seq)
    ts = _tile(2048, seq)
    tc = _tile(1024, seq)
    te = 512
    h = x.reshape(n, D_MODEL)
    for l in range(DEPTH):
        lw = layers[l]
        qa, ka, qb, kb, vt, cqk, la, cv, cgate = _in_proj(h, lw, tabs, consts, seq, tm)
        oa = lax.cond(lw["bounded"],
                      lambda *a: _diff_attn(*a, batch, seq, True),
                      lambda *a: _diff_attn(*a, batch, seq, False),
                      qa, ka, vt, lw["lam"], lw["gsub"])
        ob = lax.cond(lw["bounded"],
                      lambda *a: _gqa_attn(*a, batch, seq, True),
                      lambda *a: _gqa_attn(*a, batch, seq, False),
                      qb, kb, vt)
        gla_ok = GLA_CHUNK * jnp.max(-la) <= GLA_DECAY_BOUND
        o_f, o_b = lax.cond(
            gla_ok,
            lambda *a: (_gla(*a, batch, seq, False, True), _gla(*a, batch, seq, True, True)),
            lambda *a: (_gla(*a, batch, seq, False, False), _gla(*a, batch, seq, True, False)),
            cqk, la, cv)
        h1, xn, route, counts = _out_proj(h, oa, ob, o_f, o_b, cgate, lw, consts, tm)
        h = _moe_ple(h1, xn, route, counts, p[l].reshape(n, PLE_DIM), lw, ts, tc, te)
    return h.reshape(batch, seq, D_MODEL)


def kernel(x_prompt, x_sample, p_prompt, p_sample, g_mix, w_in, a_q_norm, a_k_norm, a_lambda, a_subln, b_q_norm, b_k_norm, c_gate_w, c_gate_b, c_out_norm, w_out, g_ffn, w_router_group, w_router_expert, w_exp_gate, w_exp_up, w_exp_down, g_ple, w_ple_gate, w_ple_proj):
    consts = _constants()
    layers = [
        _prep_layer(l, g_mix[l], w_in[l], a_q_norm[l], a_k_norm[l], a_lambda[l], a_subln[l], b_q_norm[l],
                    b_k_norm[l], c_gate_w[l], c_gate_b[l], c_out_norm[l], w_out[l], g_ffn[l],
                    w_router_group[l], w_router_expert[l], w_exp_gate[l], w_exp_up[l], w_exp_down[l],
                    g_ple[l], w_ple_gate[l], w_ple_proj[l])
        for l in range(DEPTH)
    ]
    y_prompt = _trunk(x_prompt, p_prompt, layers, consts)
    y_sample = _trunk(x_sample, p_sample, layers, consts)
    return (y_prompt, y_sample)
```
